```python
import math
import jax, jax.numpy as jnp
from jax import lax
import numpy as np

D_MODEL = 2048
BATCH = 2
SEQ = 4096
DEPTH = 4

CHUNK = 64
Q_BLOCK = 128
HEAD_DIM = 128
ROPE_THETA = 10000.0
LN_EPS = 1e-5
RMS_EPS = 1e-6
NEG_INF = -1e30

H_FOX = 8
H_CHK = 8
LEFT_CHUNKS = 8
BAND = (LEFT_CHUNKS + 1) * CHUNK
REL_MAX = 128

H_MLA = 8
Q_LORA = 512
KV_LORA = 512
MLA_NOPE = 128
MLA_ROPE = 64
MLA_V = 128
H_DIFF = 4
DIFF_DIM = 128

N_KEYS = 128
N_EXPERTS = N_KEYS * N_KEYS
PEER_HEADS = 8
PEER_TOPK = 16
PEER_DK = 256
PEER_TOKEN_BLOCK = 128

N_EVEN = (DEPTH + 1) // 2
N_ODD = DEPTH // 2
ALPHA = (2 * DEPTH) ** 0.25
BETA = (8 * DEPTH) ** -0.25

EVEN_SPLITS = (H_FOX * HEAD_DIM, H_FOX * HEAD_DIM, H_FOX * HEAD_DIM, H_FOX,
               H_CHK * HEAD_DIM, H_CHK * HEAD_DIM, H_CHK * HEAD_DIM)
ODD_SPLITS = (Q_LORA, KV_LORA, MLA_ROPE,
              H_DIFF * 2 * DIFF_DIM, H_DIFF * 2 * DIFF_DIM, H_DIFF * 2 * DIFF_DIM)
IN_EVEN = sum(EVEN_SPLITS)
IN_ODD = sum(ODD_SPLITS)
MIX_WIDTH_EVEN = (H_FOX + H_CHK) * HEAD_DIM
MIX_WIDTH_ODD = H_MLA * MLA_V + H_DIFF * 2 * DIFF_DIM

kernel_name = 'hybrid_chunk_causal_encoder'


def _split(a, sizes):
    return jnp.split(a, np.cumsum(sizes)[:-1].tolist(), axis=-1)


def layer_norm(x, g, b):
    xf = x.astype(jnp.float32)
    mu = jnp.mean(xf, axis=-1, keepdims=True)
    var = jnp.mean(jnp.square(xf - mu), axis=-1, keepdims=True)
    return ((xf - mu) * lax.rsqrt(var + LN_EPS) * g + b).astype(x.dtype)


def rms_norm(x, g):
    xf = x.astype(jnp.float32)
    return (xf * lax.rsqrt(jnp.mean(jnp.square(xf), axis=-1, keepdims=True) + RMS_EPS) * g).astype(x.dtype)


def rope(x, pos):
    d = x.shape[-1]
    inv = ROPE_THETA ** (-jnp.arange(0, d, 2, dtype=jnp.float32) / d)
    ang = pos.astype(jnp.float32)[:, None] * inv[None, :]
    cos = jnp.cos(ang)[:, None, :]
    sin = jnp.sin(ang)[:, None, :]
    xf = x.astype(jnp.float32)
    x1, x2 = xf[..., : d // 2], xf[..., d // 2:]
    return jnp.concatenate([x1 * cos - x2 * sin, x2 * cos + x1 * sin], axis=-1).astype(x.dtype)


def sweep_attention(q, k, v, scale, per_frame, log_decay=None):
    B, S, H, dk = q.shape
    nb = S // Q_BLOCK
    k_pos = jnp.arange(S)
    qb = jnp.moveaxis(q.reshape(B, nb, Q_BLOCK, H, dk), 1, 0)
    xs = (jnp.arange(nb), qb)
    if log_decay is not None:
        cum_k = jnp.transpose(log_decay, (0, 2, 1))
        xs = xs + (jnp.moveaxis(log_decay.reshape(B, nb, Q_BLOCK, H), 1, 0),)

    def block(xs_i):
        i, qi = xs_i[0], xs_i[1]
        q_pos = i * Q_BLOCK + jnp.arange(Q_BLOCK)
        s = jnp.einsum('bqhd,bkhd->bhqk', qi, k, preferred_element_type=jnp.float32) * scale
        if per_frame:
            allowed = k_pos[None, :] <= q_pos[:, None]
        else:
            allowed = (k_pos[None, :] // CHUNK) <= (q_pos[:, None] // CHUNK)
        if log_decay is not None:
            ci = jnp.transpose(xs_i[2], (0, 2, 1))
            s = s + ci[..., None] - cum_k[:, :, None, :]
        s = jnp.where(allowed, s, NEG_INF)
        p = jax.nn.softmax(s, axis=-1)
        return jnp.einsum('bhqk,bkhd->bqhd', p.astype(v.dtype), v)

    out = lax.map(block, xs)
    return jnp.moveaxis(out, 0, 1).reshape(B, S, H, v.shape[-1])


def chunk_band_attention(q, k, v, rel_bias):
    B, S, H, d = q.shape
    nc = S // CHUNK
    pad = LEFT_CHUNKS * CHUNK
    kp = jnp.pad(k, ((0, 0), (pad, 0), (0, 0), (0, 0)))
    vp = jnp.pad(v, ((0, 0), (pad, 0), (0, 0), (0, 0)))
    band_idx = jnp.arange(nc)[:, None] * CHUNK + jnp.arange(BAND)[None, :]
    kb = kp[:, band_idx]
    vb = vp[:, band_idx]
    valid = band_idx >= pad
    qc = q.reshape(B, nc, CHUNK, H, d)
    s = jnp.einsum('bcqhd,bckhd->bhcqk', qc, kb, preferred_element_type=jnp.float32) * (d ** -0.5)
    rel = (jnp.arange(CHUNK)[:, None] + pad) - jnp.arange(BAND)[None, :]
    ridx = jnp.clip(rel, -REL_MAX, REL_MAX) + REL_MAX
    bias = rel_bias.astype(jnp.float32)[:, ridx]
    s = s + bias[None, :, None, :, :]
    s = jnp.where(valid[None, None, :, None, :], s, NEG_INF)
    p = jax.nn.softmax(s, axis=-1)
    o = jnp.einsum('bhcqk,bckhd->bcqhd', p.astype(v.dtype), vb)
    return o.reshape(B, S, H, d)


def even_mixer(x, w_in, b_forget, rel_bias, w_out):
    B, S, _ = x.shape
    fq, fk, fv, f_logit, cq, ck, cv = _split(x @ w_in, EVEN_SPLITS)

    def heads(t, h):
        return t.reshape(B, S, h, HEAD_DIM)

    log_f = jax.nn.log_sigmoid((f_logit + b_forget).astype(jnp.float32))
    cum_log_f = jnp.cumsum(log_f, axis=1)
    o_fox = sweep_attention(heads(fq, H_FOX), heads(fk, H_FOX), heads(fv, H_FOX),
                            HEAD_DIM ** -0.5, True, cum_log_f)
    o_chk = chunk_band_attention(heads(cq, H_CHK), heads(ck, H_CHK), heads(cv, H_CHK), rel_bias)
    o = jnp.concatenate([o_fox.reshape(B, S, -1), o_chk.reshape(B, S, -1)], axis=-1)
    return o @ w_out


def odd_mixer(x, w_in, g_q_lora, g_kv_lora, w_uq, w_ukv, diff_lambda, g_subln, w_out, pos, layer_idx):
    B, S, _ = x.shape
    cq, ckv, kpe, dq, dk, dv = _split(x @ w_in, ODD_SPLITS)

    q = (rms_norm(cq, g_q_lora) @ w_uq).reshape(B, S, H_MLA, MLA_NOPE + MLA_ROPE)
    q = jnp.concatenate([q[..., :MLA_NOPE], rope(q[..., MLA_NOPE:], pos)], axis=-1)
    kv = (rms_norm(ckv, g_kv_lora) @ w_ukv).reshape(B, S, H_MLA, MLA_NOPE + MLA_V)
    k_pe = jnp.broadcast_to(rope(kpe[:, :, None, :], pos), (B, S, H_MLA, MLA_ROPE))
    k = jnp.concatenate([kv[..., :MLA_NOPE], k_pe], axis=-1)
    v = kv[..., MLA_NOPE:]
    o_mla = sweep_attention(q, k, v, (MLA_NOPE + MLA_ROPE) ** -0.5, False).reshape(B, S, -1)

    dq = rope(dq.reshape(B, S, H_DIFF * 2, DIFF_DIM), pos).reshape(B, S, H_DIFF, 2, DIFF_DIM)
    dk = rope(dk.reshape(B, S, H_DIFF * 2, DIFF_DIM), pos).reshape(B, S, H_DIFF, 2, DIFF_DIM)
    dv = dv.reshape(B, S, H_DIFF, 2 * DIFF_DIM)
    a1 = sweep_attention(dq[:, :, :, 0], dk[:, :, :, 0], dv, DIFF_DIM ** -0.5, False)
    a2 = sweep_attention(dq[:, :, :, 1], dk[:, :, :, 1], dv, DIFF_DIM ** -0.5, False)
    lam_init = 0.8 - 0.6 * math.exp(-0.3 * layer_idx)
    lf = diff_lambda.astype(jnp.float32)
    lam = jnp.exp(jnp.sum(lf[0] * lf[1])) - jnp.exp(jnp.sum(lf[2] * lf[3])) + lam_init
    o_diff = rms_norm(a1 - lam.astype(a1.dtype) * a2, g_subln) * (1.0 - lam_init)
    o_diff = o_diff.reshape(B, S, -1)

    o = jnp.concatenate([o_mla, o_diff], axis=-1)
    return o @ w_out


def peer_ffn(x, w_query, sub_keys, u_tab, v_tab):
    B, S, D = x.shape
    T = B * S
    xt = x.reshape(T, D)
    q = (xt @ w_query).reshape(T, PEER_HEADS, 2, PEER_DK // 2)
    s = jnp.einsum('thpd,hpnd->thpn', q, sub_keys, preferred_element_type=jnp.float32)
    sv, si = lax.top_k(s, PEER_TOPK)
    cand = sv[:, :, 0, :, None] + sv[:, :, 1, None, :]
    cidx = si[:, :, 0, :, None] * N_KEYS + si[:, :, 1, None, :]
    top_s, top_pos = lax.top_k(cand.reshape(T, PEER_HEADS, PEER_TOPK * PEER_TOPK), PEER_TOPK)
    idx = jnp.take_along_axis(cidx.reshape(T, PEER_HEADS, PEER_TOPK * PEER_TOPK), top_pos, axis=-1)
    g = jax.nn.softmax(top_s, axis=-1)
    E = PEER_HEADS * PEER_TOPK
    nb = T // PEER_TOKEN_BLOCK

    def block(args):
        xb, ib, gb = args
        h = jax.nn.gelu(jnp.einsum('td,ted->te', xb, u_tab[ib], preferred_element_type=jnp.float32))
        w = (gb * h).astype(v_tab.dtype)
        return jnp.einsum('te,ted->td', w, v_tab[ib])

    out = lax.map(block, (xt.reshape(nb, PEER_TOKEN_BLOCK, D),
                          idx.reshape(nb, PEER_TOKEN_BLOCK, E),
                          g.reshape(nb, PEER_TOKEN_BLOCK, E)))
    return out.reshape(B, S, D)


def setup_inputs(seed: int = 0) -> dict:
    key = jax.random.key(seed)
    ks = jax.random.split(key, 21)
    d = D_MODEL

    def nrm(k, shape, scale):
        return jax.random.normal(k, shape, jnp.float32) * scale

    return {
        'x': nrm(ks[0], (BATCH, SEQ, d), 1.0),
        'w_in_even': nrm(ks[1], (N_EVEN, d, IN_EVEN), d ** -0.5),
        'b_forget': jax.random.uniform(ks[2], (N_EVEN, H_FOX), jnp.float32, 1.0, 4.0),
        'rel_bias': nrm(ks[3], (N_EVEN, H_CHK, 2 * REL_MAX + 1), 0.1),
        'w_out_even': nrm(ks[4], (N_EVEN, MIX_WIDTH_EVEN, d), BETA * MIX_WIDTH_EVEN ** -0.5),
        'w_in_odd': nrm(ks[5], (N_ODD, d, IN_ODD), d ** -0.5),
        'g_q_lora': 1.0 + nrm(ks[6], (N_ODD, Q_LORA), 0.02),
        'g_kv_lora': 1.0 + nrm(ks[7], (N_ODD, KV_LORA), 0.02),
        'w_uq': nrm(ks[8], (N_ODD, Q_LORA, H_MLA * (MLA_NOPE + MLA_ROPE)), Q_LORA ** -0.5),
        'w_ukv': nrm(ks[9], (N_ODD, KV_LORA, H_MLA * (MLA_NOPE + MLA_V)), KV_LORA ** -0.5),
        'diff_lambda': nrm(ks[10], (N_ODD, 4, DIFF_DIM), 0.1),
        'g_subln': 1.0 + nrm(ks[11], (N_ODD, 2 * DIFF_DIM), 0.02),
        'w_out_odd': nrm(ks[12], (N_ODD, MIX_WIDTH_ODD, d), BETA * MIX_WIDTH_ODD ** -0.5),
        'peer_w_query': nrm(ks[13], (DEPTH, d, PEER_HEADS * PEER_DK), d ** -0.5),
        'peer_sub_keys': nrm(ks[14], (DEPTH, PEER_HEADS, 2, N_KEYS, PEER_DK // 2), (PEER_DK // 2) ** -0.5),
        'peer_u': nrm(ks[15], (DEPTH, N_EXPERTS, d), d ** -0.5),
        'peer_v': nrm(ks[16], (DEPTH, N_EXPERTS, d), BETA * (PEER_HEADS * PEER_TOPK) ** -0.5),
        'ln_mix_g': 1.0 + nrm(ks[17], (DEPTH, d), 0.02),
        'ln_mix_b': nrm(ks[18], (DEPTH, d), 0.02),
        'ln_ffn_g': 1.0 + nrm(ks[19], (DEPTH, d), 0.02),
        'ln_ffn_b': nrm(ks[20], (DEPTH, d), 0.02),
    }


def reference(x, w_in_even, b_forget, rel_bias, w_out_even, w_in_odd, g_q_lora, g_kv_lora,
              w_uq, w_ukv, diff_lambda, g_subln, w_out_odd, peer_w_query, peer_sub_keys,
              peer_u, peer_v, ln_mix_g, ln_mix_b, ln_ffn_g, ln_ffn_b):
    pos = jnp.arange(x.shape[1])
    for l in range(DEPTH):
        i = l // 2
        if l % 2 == 0:
            y = even_mixer(x, w_in_even[i], b_forget[i], rel_bias[i], w_out_even[i])
        else:
            y = odd_mixer(x, w_in_odd[i], g_q_lora[i], g_kv_lora[i], w_uq[i], w_ukv[i],
                          diff_lambda[i], g_subln[i], w_out_odd[i], pos, l)
        x = layer_norm(ALPHA * x + y, ln_mix_g[l], ln_mix_b[l])
        y = peer_ffn(x, peer_w_query[l], peer_sub_keys[l], peer_u[l], peer_v[l])
        x = layer_norm(ALPHA * x + y, ln_ffn_g[l], ln_ffn_b[l])
    return x
```

```python
import functools
import math

import numpy as np
import jax
import jax.numpy as jnp
from jax import lax
from jax.experimental import pallas as pl
from jax.experimental.pallas import tpu as pltpu

F32 = jnp.float32
BF16 = jnp.bfloat16

CHUNK = 64
HEAD_DIM = 128
ROPE_THETA = 10000.0
LN_EPS = 1e-5
RMS_EPS = 1e-6
NEG_INF = -1e30
H_FOX = 8
H_CHK = 8
LEFT_CHUNKS = 8
REL_MAX = 128
H_MLA = 8
Q_LORA = 512
KV_LORA = 512
MLA_NOPE = 128
MLA_ROPE = 64
MLA_V = 128
H_DIFF = 4
DIFF_DIM = 128
N_KEYS = 128
PEER_HEADS = 8
PEER_TOPK = 16
PEER_HALF = 128

LANES = 128
VMEM_LIMIT = 56 * 1024 * 1024


def _cparams(sem):
    return pltpu.CompilerParams(dimension_semantics=sem, vmem_limit_bytes=VMEM_LIMIT)


def _mm_kernel(*refs, has_gain, has_scale, rope_shifts, n_tabs):
    it = iter(refs)
    x_ref = next(it)
    w_ref = next(it)
    g_ref = next(it) if has_gain else None
    sc_ref = next(it) if has_scale else None
    tabs = [next(it) for _ in range(n_tabs)]
    o_ref = next(it)

    x = x_ref[...]
    if has_gain:
        xf = x.astype(F32)
        xf = xf * lax.rsqrt(jnp.mean(xf * xf, axis=-1, keepdims=True) + RMS_EPS) * g_ref[...]
        x = xf
    x = x.astype(BF16)
    acc = jnp.dot(x, w_ref[...], preferred_element_type=F32)
    if has_scale:
        acc = acc * sc_ref[...]
    if n_tabs:
        t = [r[...] for r in tabs]
        for j in range(acc.shape[1] // LANES):
            y = acc[:, j * LANES:(j + 1) * LANES]
            out = y * t[0]
            for k, s in enumerate(rope_shifts):
                out = out + pltpu.roll(y, s, 1) * t[k + 1]
            o_ref[:, j * LANES:(j + 1) * LANES] = out.astype(o_ref.dtype)
    else:
        o_ref[...] = acc.astype(o_ref.dtype)


def _mm(x, w, *, out_dtype, gain=None, col_scale=None, rope=None, bm=1024, bn=1024):
    M, K = x.shape
    N = w.shape[1]
    bm = min(bm, M)
    bn = min(bn, N)
    if rope is not None:
        bm = min(bm, rope[1][0].shape[0])
    assert M % bm == 0 and N % bn == 0
    in_specs = [pl.BlockSpec((bm, K), lambda i, j: (i, 0)),
                pl.BlockSpec((K, bn), lambda i, j: (0, j))]
    args = [x, w]
    if gain is not None:
        in_specs.append(pl.BlockSpec((1, K), lambda i, j: (0, 0)))
        args.append(gain.reshape(1, K).astype(F32))
    if col_scale is not None:
        in_specs.append(pl.BlockSpec((1, bn), lambda i, j: (0, j)))
        args.append(col_scale.reshape(1, N).astype(F32))
    shifts, tabs = ((), ())
    if rope is not None:
        shifts, tabs = rope
        seq = tabs[0].shape[0]
        assert seq % bm == 0
        nsb = seq // bm
        for t in tabs:
            in_specs.append(pl.BlockSpec((bm, LANES), lambda i, j: (i % nsb, 0)))
            args.append(t)
    kern = functools.partial(_mm_kernel, has_gain=gain is not None,
                             has_scale=col_scale is not None,
                             rope_shifts=tuple(shifts), n_tabs=len(tabs))
    return pl.pallas_call(
        kern,
        out_shape=jax.ShapeDtypeStruct((M, N), out_dtype),
        grid=(M // bm, N // bn),
        in_specs=in_specs,
        out_specs=pl.BlockSpec((bm, bn), lambda i, j: (i, j)),
        compiler_params=_cparams(("parallel", "parallel")),
        name="mm",
    )(*args)


def _res_ln_kernel(x_ref, y_ref, g_ref, b_ref, o_ref, ob_ref, *, alpha):
    z = alpha * x_ref[...] + y_ref[...]
    mu = jnp.mean(z, axis=-1, keepdims=True)
    zc = z - mu
    var = jnp.mean(zc * zc, axis=-1, keepdims=True)
    out = zc * lax.rsqrt(var + LN_EPS) * g_ref[...] + b_ref[...]
    o_ref[...] = out
    ob_ref[...] = out.astype(BF16)


def _res_ln(x, y, g, b, alpha, bm=512):
    M, D = x.shape
    bm = min(bm, M)
    row = pl.BlockSpec((bm, D), lambda i: (i, 0))
    vec = pl.BlockSpec((1, D), lambda i: (0, 0))
    return pl.pallas_call(
        functools.partial(_res_ln_kernel, alpha=alpha),
        out_shape=(jax.ShapeDtypeStruct((M, D), F32), jax.ShapeDtypeStruct((M, D), BF16)),
        grid=(M // bm,),
        in_specs=[row, row, vec, vec],
        out_specs=(row, row),
        compiler_params=_cparams(("parallel",)),
        name="res_ln",
    )(x, y, g.reshape(1, D), b.reshape(1, D))


SCAN_BLOCK = 256


def _gate_scan_kernel(z_ref, b_ref, o_ref):
    seq = z_ref.shape[0]
    r = lax.broadcasted_iota(jnp.int32, (SCAN_BLOCK, SCAN_BLOCK), 0)
    c = lax.broadcasted_iota(jnp.int32, (SCAN_BLOCK, SCAN_BLOCK), 1)
    tri = jnp.where(c <= r, 1.0, 0.0).astype(F32)
    carry = jnp.zeros((1, LANES), F32)
    for i in range(seq // SCAN_BLOCK):
        z = z_ref[i * SCAN_BLOCK:(i + 1) * SCAN_BLOCK, :] + b_ref[...]
        logf = jnp.minimum(z, 0.0) - jnp.log(1.0 + jnp.exp(-jnp.abs(z)))
        cs = jnp.dot(tri, logf, preferred_element_type=F32,
                     precision=lax.Precision.HIGHEST) + carry
        o_ref[i * SCAN_BLOCK:(i + 1) * SCAN_BLOCK, :] = cs
        carry = cs[SCAN_BLOCK - 1:SCAN_BLOCK, :]


def _gate_scan(z, bias_row, batch, seq):
    assert seq % SCAN_BLOCK == 0
    return pl.pallas_call(
        _gate_scan_kernel,
        out_shape=jax.ShapeDtypeStruct((batch * seq, LANES), F32),
        grid=(batch,),
        in_specs=[pl.BlockSpec((seq, LANES), lambda b: (b, 0)),
                  pl.BlockSpec((1, LANES), lambda b: (0, 0))],
        out_specs=pl.BlockSpec((seq, LANES), lambda b: (b, 0)),
        compiler_params=_cparams(("parallel",)),
        name="gate_scan",
    )(z, bias_row)


def _attn_kernel(*refs, mode, bq, bk, has_q2, has_decay, has_bias):
    qt, kt, ft, lt = refs[:4]
    it = iter(refs[4:])
    q_ref = next(it)
    k_ref = next(it)
    v_ref = next(it)
    q2_ref = next(it) if has_q2 else None
    k2_ref = next(it) if has_q2 else None
    cq_ref = next(it) if has_decay else None
    ck_ref = next(it) if has_decay else None
    bias_ref = next(it) if has_bias else None
    o_ref = next(it)
    m_sc = next(it)
    l_sc = next(it)
    acc_sc = next(it)

    p = pl.program_id(2)
    qi = qt[p]
    ki = kt[p]

    @pl.when(ft[p] == 1)
    def _():
        m_sc[...] = jnp.full(m_sc.shape, NEG_INF, F32)
        l_sc[...] = jnp.zeros(l_sc.shape, F32)
        acc_sc[...] = jnp.zeros(acc_sc.shape, F32)

    nt = (((1,), (1,)), ((), ()))
    s = lax.dot_general(q_ref[...], k_ref[...], nt, preferred_element_type=F32)
    if has_q2:
        s = s + lax.dot_general(q2_ref[...], k2_ref[...], nt, preferred_element_type=F32)
    if has_decay:
        s = s + (cq_ref[0] - ck_ref[0])
    if has_bias:
        s = s + bias_ref[0, 0]
    rows = qi * bq + lax.broadcasted_iota(jnp.int32, (bq, bk), 0)
    cols = ki * bk + lax.broadcasted_iota(jnp.int32, (bq, bk), 1)
    if mode == "causal":
        allowed = cols <= rows
    else:
        qc = lax.shift_right_logical(rows, 6)
        kc = lax.shift_right_logical(cols, 6)
        if mode == "chunk":
            allowed = kc <= qc
        else:
            dist = qc - kc
            allowed = dist * (LEFT_CHUNKS - dist) >= 0
    s = jnp.where(allowed, s, NEG_INF)

    m_prev = m_sc[...]
    m_new = jnp.maximum(m_prev, jnp.max(s, axis=1, keepdims=True))
    alpha = jnp.exp(m_prev - m_new)
    pr = jnp.exp(s - m_new)
    l_sc[...] = alpha * l_sc[...] + jnp.sum(pr, axis=1, keepdims=True)
    acc_sc[...] = alpha * acc_sc[...] + jnp.dot(pr.astype(BF16), v_ref[...],
                                                preferred_element_type=F32)
    m_sc[...] = m_new

    @pl.when(lt[p] == 1)
    def _():
        o_ref[...] = (acc_sc[...] / l_sc[...]).astype(o_ref.dtype)


def _pair_tables(nq, mode, bq, bk):
    qs, ks = [], []
    for qi in range(nq):
        if mode == "band":
            lo = max(0, (qi * bq - LEFT_CHUNKS * CHUNK) // bk)
        else:
            lo = 0
        hi = ((qi + 1) * bq - 1) // bk
        for ki in range(lo, hi + 1):
            qs.append(qi)
            ks.append(ki)
    n = len(qs)
    first = [1 if (i == 0 or qs[i - 1] != qs[i]) else 0 for i in range(n)]
    last = [1 if (i == n - 1 or qs[i + 1] != qs[i]) else 0 for i in range(n)]
    as_i32 = lambda a: jnp.asarray(np.array(a, dtype=np.int32))
    return as_i32(qs), as_i32(ks), as_i32(first), as_i32(last), n


def _attention(q, k, v, *, batch, seq, heads, q_off, k_off, v_off, dv, v_group, mode,
               bq, bk, out_dtype, q2=None, k2=None, q2_off=0, cq=None, ck=None, bias=None):
    bq = min(bq, seq)
    bk = min(bk, seq)
    nq = seq // bq
    nk = seq // bk
    qt, kt, ft, lt, npairs = _pair_tables(nq, mode, bq, bk)
    has_q2 = q2 is not None
    has_decay = cq is not None
    has_bias = bias is not None

    in_specs = [
        pl.BlockSpec((bq, HEAD_DIM), lambda b, h, p, qt, kt, ft, lt: (b * nq + qt[p], q_off + h)),
        pl.BlockSpec((bk, HEAD_DIM), lambda b, h, p, qt, kt, ft, lt: (b * nk + kt[p], k_off + h)),
        pl.BlockSpec((bk, dv), lambda b, h, p, qt, kt, ft, lt: (b * nk + kt[p], v_off + h // v_group)),
    ]
    args = [q, k, v]
    if has_q2:
        in_specs += [
            pl.BlockSpec((bq, HEAD_DIM), lambda b, h, p, qt, kt, ft, lt: (b * nq + qt[p], q2_off + h)),
            pl.BlockSpec((bk, HEAD_DIM), lambda b, h, p, qt, kt, ft, lt: (b * nk + kt[p], 0)),
        ]
        args += [q2, k2]
    if has_decay:
        in_specs += [
            pl.BlockSpec((1, bq, 1), lambda b, h, p, qt, kt, ft, lt: (b * heads + h, qt[p], 0)),
            pl.BlockSpec((1, 1, bk), lambda b, h, p, qt, kt, ft, lt: (b * heads + h, 0, kt[p])),
        ]
        args += [cq, ck]
    if has_bias:
        in_specs.append(
            pl.BlockSpec((1, 1, bq, bk), lambda b, h, p, qt, kt, ft, lt: (h, qt[p] - kt[p], 0, 0)))
        args.append(bias)

    kern = functools.partial(_attn_kernel, mode=mode, bq=bq, bk=bk, has_q2=has_q2,
                             has_decay=has_decay, has_bias=has_bias)
    return pl.pallas_call(
        kern,
        out_shape=jax.ShapeDtypeStruct((batch * seq, heads * dv), out_dtype),
        grid_spec=pltpu.PrefetchScalarGridSpec(
            num_scalar_prefetch=4,
            grid=(batch, heads, npairs),
            in_specs=in_specs,
            out_specs=pl.BlockSpec((bq, dv), lambda b, h, p, qt, kt, ft, lt: (b * nq + qt[p], h)),
            scratch_shapes=[pltpu.VMEM((bq, 1), F32), pltpu.VMEM((bq, 1), F32),
                            pltpu.VMEM((bq, dv), F32)],
        ),
        compiler_params=_cparams(("parallel", "parallel", "arbitrary")),
        name="attn_" + mode,
    )(qt, kt, ft, lt, *args)


def _diff_combine_kernel(a_ref, lam_ref, g_ref, o_ref, *, lam_init):
    lf = lam_ref[...]
    lam = (jnp.exp(jnp.sum(lf[0:1] * lf[1:2], axis=-1, keepdims=True))
           - jnp.exp(jnp.sum(lf[2:3] * lf[3:4], axis=-1, keepdims=True)) + lam_init)
    w = 2 * DIFF_DIM
    for hd in range(H_DIFF):
        a1 = a_ref[:, (2 * hd) * w:(2 * hd + 1) * w]
        a2 = a_ref[:, (2 * hd + 1) * w:(2 * hd + 2) * w]
        d = a1 - lam * a2
        d = d * lax.rsqrt(jnp.mean(d * d, axis=-1, keepdims=True) + RMS_EPS) * g_ref[...]
        o_ref[:, hd * w:(hd + 1) * w] = (d * (1.0 - lam_init)).astype(o_ref.dtype)


def _diff_combine(a, diff_lambda, g_subln, lam_init, bm=512):
    M = a.shape[0]
    bm = min(bm, M)
    w = 2 * DIFF_DIM
    return pl.pallas_call(
        functools.partial(_diff_combine_kernel, lam_init=lam_init),
        out_shape=jax.ShapeDtypeStruct((M, H_DIFF * w), BF16),
        grid=(M // bm,),
        in_specs=[pl.BlockSpec((bm, 2 * H_DIFF * w), lambda i: (i, 0)),
                  pl.BlockSpec((4, DIFF_DIM), lambda i: (0, 0)),
                  pl.BlockSpec((1, w), lambda i: (0, 0))],
        out_specs=pl.BlockSpec((bm, H_DIFF * w), lambda i: (i, 0)),
        compiler_params=_cparams(("parallel",)),
        name="diff_combine",
    )(a, diff_lambda.astype(F32), g_subln.reshape(1, w).astype(F32))


PEER_SUB = 256


def _top_values(s, n):
    vals = []
    work = s
    for r in range(n):
        m = jnp.max(work, axis=0, keepdims=True)
        vals.append(m)
        if r + 1 < n:
            work = jnp.where(work >= m, -jnp.inf, work)
    return vals


def _peer_stats_kernel(q_ref, keys_ref, s2_ref, c_ref, tau_ref):
    nt = (((1,), (1,)), ((), ()))
    tb = q_ref.shape[0]
    k1 = keys_ref[0, 0]
    k2 = keys_ref[0, 1]
    for t0 in range(0, tb, PEER_SUB):
        q1 = q_ref[t0:t0 + PEER_SUB, 0:PEER_HALF]
        q2 = q_ref[t0:t0 + PEER_SUB, PEER_HALF:2 * PEER_HALF]
        s1 = lax.dot_general(k1, q1, nt, preferred_element_type=F32)
        s2 = lax.dot_general(k2, q2, nt, preferred_element_type=F32)
        sv1 = _top_values(s1, PEER_TOPK)
        sv2 = jnp.concatenate(_top_values(s2, PEER_TOPK), axis=0)
        cand = jnp.concatenate([sv1[a] + sv2 for a in range(PEER_TOPK)], axis=0)
        top = _top_values(cand, PEER_TOPK + 1)
        cmax = top[0]
        tau = 0.5 * (top[PEER_TOPK - 1] + top[PEER_TOPK])
        z = jnp.sum(jnp.where(cand >= tau, jnp.exp(cand - cmax), 0.0), axis=0, keepdims=True)
        lse = cmax + jnp.log(z)
        s2_ref[0, :, t0:t0 + PEER_SUB] = s2
        c_ref[0, :, t0:t0 + PEER_SUB] = s1 - lse
        tau_ref[0, :, t0:t0 + PEER_SUB] = tau - lse


def _peer_stats(q, keys, tb=1024):
    T = q.shape[0]
    tb = min(tb, T)
    assert T % tb == 0 and tb % PEER_SUB == 0
    big = jax.ShapeDtypeStruct((PEER_HEADS, N_KEYS, T), F32)
    return pl.pallas_call(
        _peer_stats_kernel,
        out_shape=(big, big, jax.ShapeDtypeStruct((PEER_HEADS, 1, T), F32)),
        grid=(T // tb, PEER_HEADS),
        in_specs=[pl.BlockSpec((tb, 2 * PEER_HALF), lambda i, h: (i, h)),
                  pl.BlockSpec((1, 2, N_KEYS, PEER_HALF), lambda i, h: (h, 0, 0, 0))],
        out_specs=(pl.BlockSpec((1, N_KEYS, tb), lambda i, h: (h, 0, i)),
                   pl.BlockSpec((1, N_KEYS, tb), lambda i, h: (h, 0, i)),
                   pl.BlockSpec((1, 1, tb), lambda i, h: (h, 0, i))),
        compiler_params=_cparams(("parallel", "parallel")),
        name="peer_stats",
    )(q, keys)


def _gelu_tanh(x):
    return 0.5 * x * (1.0 + jnp.tanh(math.sqrt(2.0 / math.pi) * (x + 0.044715 * (x * x * x))))


def _peer_mix_kernel(x_ref, u_ref, v_ref, s2_ref, c_ref, tau_ref, o_ref, w_sc, *, ni, tsub):
    ci = pl.program_id(1)
    tb = x_ref.shape[0]

    @pl.when(ci == 0)
    def _():
        o_ref[...] = jnp.zeros(o_ref.shape, F32)

    nt = (((1,), (1,)), ((), ()))
    tn = (((0,), (0,)), ((), ()))
    for t0 in range(0, tb, tsub):
        ht = lax.dot_general(u_ref[...], x_ref[t0:t0 + tsub, :], nt, preferred_element_type=F32)
        for il in range(ni):
            g = jnp.zeros((N_KEYS, tsub), F32)
            for h in range(PEER_HEADS):
                crow = c_ref[h, pl.ds(ci * ni + il, 1), t0:t0 + tsub]
                z = s2_ref[h, :, t0:t0 + tsub] + crow
                g = g + jnp.where(z >= tau_ref[h, :, t0:t0 + tsub], jnp.exp(z), 0.0)
            hh = ht[il * N_KEYS:(il + 1) * N_KEYS, :]
            w_sc[il * N_KEYS:(il + 1) * N_KEYS, :] = (g * _gelu_tanh(hh)).astype(BF16)
        o_ref[t0:t0 + tsub, :] += lax.dot_general(w_sc[...], v_ref[...], tn,
                                                  preferred_element_type=F32)


def _peer_mix(xb, u, v, s2, c, tau, *, tb=1024, ec=512, tsub=256):
    T, D = xb.shape
    E = u.shape[0]
    tb = min(tb, T)
    tsub = min(tsub, tb)
    assert T % tb == 0 and E % ec == 0 and ec % N_KEYS == 0 and tb % tsub == 0
    ni = ec // N_KEYS
    return pl.pallas_call(
        functools.partial(_peer_mix_kernel, ni=ni, tsub=tsub),
        out_shape=jax.ShapeDtypeStruct((T, D), F32),
        grid=(T // tb, E // ec),
        in_specs=[pl.BlockSpec((tb, D), lambda i, e: (i, 0)),
                  pl.BlockSpec((ec, D), lambda i, e: (e, 0)),
                  pl.BlockSpec((ec, D), lambda i, e: (e, 0)),
                  pl.BlockSpec((PEER_HEADS, N_KEYS, tb), lambda i, e: (0, 0, i)),
                  pl.BlockSpec((PEER_HEADS, N_KEYS, tb), lambda i, e: (0, 0, i)),
                  pl.BlockSpec((PEER_HEADS, 1, tb), lambda i, e: (0, 0, i))],
        out_specs=pl.BlockSpec((tb, D), lambda i, e: (i, 0)),
        scratch_shapes=[pltpu.VMEM((ec, tsub), BF16)],
        compiler_params=_cparams(("parallel", "arbitrary")),
        name="peer_mix",
    )(xb, u, v, s2, c, tau)


def _rope_tables(seq, d):
    inv = ROPE_THETA ** (-jnp.arange(0, d, 2, dtype=F32) / d)
    ang = jnp.arange(seq, dtype=F32)[:, None] * inv[None, :]
    cos, sin = jnp.cos(ang), jnp.sin(ang)
    half = d // 2
    pad = jnp.zeros((seq, LANES - d), F32)
    t0 = jnp.concatenate([cos, cos, pad], axis=1)
    if d == LANES:
        return (half,), (t0, jnp.concatenate([-sin, sin], axis=1))
    zh = jnp.zeros((seq, half), F32)
    t1 = jnp.concatenate([-sin, zh, pad], axis=1)
    t2 = jnp.concatenate([zh, sin, pad], axis=1)
    return (LANES - half, half), (t0, t1, t2)


def _band_bias_tiles(rel_bias, bq, bk):
    nd = (LEFT_CHUNKS * CHUNK + bq - 1) // bk + 1
    tl = np.arange(bq)[:, None]
    sl = np.arange(bk)[None, :]
    tiles = []
    for dlt in range(nd):
        ridx = np.clip(dlt * bk + tl - sl, -REL_MAX, REL_MAX) + REL_MAX
        tiles.append(jnp.take(rel_bias.astype(F32), jnp.asarray(ridx.reshape(-1)), axis=1)
                     .reshape(rel_bias.shape[0], bq, bk))
    return jnp.stack(tiles, axis=1)


def _even_layer(xb, w_in, b_forget, rel_bias, w_out, batch, seq):
    T = xb.shape[0]
    hd = HEAD_DIM
    scale = hd ** -0.5
    n = H_FOX * hd
    fq, fk, fv = w_in[:, 0:n], w_in[:, n:2 * n], w_in[:, 2 * n:3 * n]
    wf = w_in[:, 3 * n:3 * n + H_FOX]
    o = 3 * n + H_FOX
    m = H_CHK * hd
    cq, ck, cv = w_in[:, o:o + m], w_in[:, o + m:o + 2 * m], w_in[:, o + 2 * m:o + 3 * m]
    w_main = jnp.concatenate([fq, cq, fk, fv, ck, cv], axis=1).astype(BF16)
    col_scale = jnp.concatenate([jnp.full((n + m,), scale, F32), jnp.ones((2 * n + 2 * m,), F32)])
    proj = _mm(xb, w_main, out_dtype=BF16, col_scale=col_scale)

    wf_pad = jnp.pad(wf, ((0, 0), (0, LANES - H_FOX))).astype(BF16)
    logits = _mm(xb, wf_pad, out_dtype=F32)
    bias_row = jnp.pad(b_forget.astype(F32), (0, LANES - H_FOX)).reshape(1, LANES)
    cum = _gate_scan(logits, bias_row, batch, seq)
    cum = cum[:, :H_FOX].reshape(batch, seq, H_FOX).transpose(0, 2, 1)
    cq_col = cum.reshape(batch * H_FOX, seq, 1)
    ck_row = cum.reshape(batch * H_FOX, 1, seq)

    nh = n // hd
    o_fox = _attention(proj, proj, proj, batch=batch, seq=seq, heads=H_FOX,
                       q_off=0, k_off=2 * nh, v_off=3 * nh, dv=hd, v_group=1, mode="causal",
                       bq=512, bk=512, out_dtype=BF16, cq=cq_col, ck=ck_row)
    bqb = min(256, seq)
    bias = _band_bias_tiles(rel_bias, bqb, bqb)
    o_chk = _attention(proj, proj, proj, batch=batch, seq=seq, heads=H_CHK,
                       q_off=nh, k_off=4 * nh, v_off=5 * nh, dv=hd, v_group=1, mode="band",
                       bq=bqb, bk=bqb, out_dtype=BF16, bias=bias)
    o_cat = jnp.concatenate([o_fox, o_chk], axis=1)
    return _mm(o_cat, w_out.astype(BF16), out_dtype=F32)


def _odd_layer(xb, w_in, g_q_lora, g_kv_lora, w_uq, w_ukv, diff_lambda, g_subln, w_out,
               layer_idx, batch, seq):
    hd = DIFF_DIM
    o1 = Q_LORA + KV_LORA
    o2 = o1 + MLA_ROPE
    nd = H_DIFF * 2 * DIFF_DIM
    w_lora = w_in[:, :o1].astype(BF16)
    w_kpe = jnp.pad(w_in[:, o1:o2], ((0, 0), (0, LANES - MLA_ROPE))).astype(BF16)
    w_dqk = w_in[:, o2:o2 + 2 * nd].astype(BF16)
    w_dv = w_in[:, o2 + 2 * nd:o2 + 3 * nd].astype(BF16)

    rope128 = _rope_tables(seq, DIFF_DIM)
    rope64 = _rope_tables(seq, MLA_ROPE)

    lora = _mm(xb, w_lora, out_dtype=F32)
    kpe = _mm(xb, w_kpe, out_dtype=BF16, rope=rope64)
    dscale = jnp.concatenate([jnp.full((nd,), DIFF_DIM ** -0.5, F32), jnp.ones((nd,), F32)])
    dqk = _mm(xb, w_dqk, out_dtype=BF16, col_scale=dscale, rope=rope128)
    dv = _mm(xb, w_dv, out_dtype=BF16)

    mscale = (MLA_NOPE + MLA_ROPE) ** -0.5
    wq = w_uq.reshape(Q_LORA, H_MLA, MLA_NOPE + MLA_ROPE)
    wq_nope = wq[:, :, :MLA_NOPE].reshape(Q_LORA, H_MLA * MLA_NOPE).astype(BF16)
    wq_rope = jnp.pad(wq[:, :, MLA_NOPE:], ((0, 0), (0, 0), (0, LANES - MLA_ROPE)))
    wq_rope = wq_rope.reshape(Q_LORA, H_MLA * LANES).astype(BF16)
    cq = lora[:, :Q_LORA]
    ckv = lora[:, Q_LORA:]
    ms = jnp.full((H_MLA * LANES,), mscale, F32)
    q_nope = _mm(cq, wq_nope, out_dtype=BF16, gain=g_q_lora, col_scale=ms)
    q_rope = _mm(cq, wq_rope, out_dtype=BF16, gain=g_q_lora, col_scale=ms, rope=rope64)
    wkv = w_ukv.reshape(KV_LORA, H_MLA, MLA_NOPE + MLA_V)
    wkv = jnp.concatenate([wkv[:, :, :MLA_NOPE].reshape(KV_LORA, H_MLA * MLA_NOPE),
                           wkv[:, :, MLA_NOPE:].reshape(KV_LORA, H_MLA * MLA_V)], axis=1)
    kv = _mm(ckv, wkv.astype(BF16), out_dtype=BF16, gain=g_kv_lora)

    o_mla = _attention(q_nope, kv, kv, batch=batch, seq=seq, heads=H_MLA,
                       q_off=0, k_off=0, v_off=H_MLA, dv=MLA_V, v_group=1, mode="chunk",
                       bq=512, bk=512, out_dtype=BF16, q2=q_rope, k2=kpe, q2_off=0)
    a = _attention(dqk, dqk, dv, batch=batch, seq=seq, heads=2 * H_DIFF,
                   q_off=0, k_off=2 * H_DIFF, v_off=0, dv=2 * DIFF_DIM, v_group=2, mode="chunk",
                   bq=512, bk=512, out_dtype=F32)
    lam_init = 0.8 - 0.6 * math.exp(-0.3 * layer_idx)
    o_diff = _diff_combine(a, diff_lambda, g_subln, lam_init)
    o_cat = jnp.concatenate([o_mla, o_diff], axis=1)
    return _mm(o_cat, w_out.astype(BF16), out_dtype=F32)


def _peer_layer(xb, w_query, sub_keys, u_tab, v_tab):
    q = _mm(xb, w_query.astype(BF16), out_dtype=BF16)
    s2, c, tau = _peer_stats(q, sub_keys.astype(BF16))
    return _peer_mix(xb, u_tab.astype(BF16), v_tab.astype(BF16), s2, c, tau)


def kernel(x, w_in_even, b_forget, rel_bias, w_out_even, w_in_odd, g_q_lora, g_kv_lora, w_uq, w_ukv, diff_lambda, g_subln, w_out_odd, peer_w_query, peer_sub_keys, peer_u, peer_v, ln_mix_g, ln_mix_b, ln_ffn_g, ln_ffn_b):
    batch, seq, d = x.shape
    depth = ln_mix_g.shape[0]
    alpha = (2 * depth) ** 0.25
    xf = x.reshape(batch * seq, d).astype(F32)
    xb = xf.astype(BF16)
    for l in range(depth):
        i = l // 2
        if l % 2 == 0:
            y = _even_layer(xb, w_in_even[i], b_forget[i], rel_bias[i], w_out_even[i], batch, seq)
        else:
            y = _odd_layer(xb, w_in_odd[i], g_q_lora[i], g_kv_lora[i], w_uq[i], w_ukv[i],
                           diff_lambda[i], g_subln[i], w_out_odd[i], l, batch, seq)
        xf, xb = _res_ln(xf, y, ln_mix_g[l], ln_mix_b[l], alpha)
        y = _peer_layer(xb, peer_w_query[l], peer_sub_keys[l], peer_u[l], peer_v[l])
        xf, xb = _res_ln(xf, y, ln_ffn_g[l], ln_ffn_b[l], alpha)
    return xf.reshape(batch, seq, d)
```

```python
import functools
import math

import numpy as np
import jax
import jax.numpy as jnp
from jax import lax
from jax.experimental import pallas as pl
from jax.experimental.pallas import tpu as pltpu

F32 = jnp.float32
BF16 = jnp.bfloat16

CHUNK = 64
HEAD_DIM = 128
ROPE_THETA = 10000.0
LN_EPS = 1e-5
RMS_EPS = 1e-6
NEG_INF = -1e30
H_FOX = 8
H_CHK = 8
LEFT_CHUNKS = 8
REL_MAX = 128
H_MLA = 8
Q_LORA = 512
KV_LORA = 512
MLA_NOPE = 128
MLA_ROPE = 64
MLA_V = 128
H_DIFF = 4
DIFF_DIM = 128
N_KEYS = 128
PEER_HEADS = 8
PEER_TOPK = 16
PEER_HALF = 128

LOG2E = math.log2(math.e)
LANES = 128
VMEM_LIMIT = 56 * 1024 * 1024


def _cparams(sem):
    return pltpu.CompilerParams(dimension_semantics=sem, vmem_limit_bytes=VMEM_LIMIT)


def _mm_kernel(*refs, has_gain, has_scale, rope_shifts, n_tabs):
    it = iter(refs)
    x_ref = next(it)
    w_ref = next(it)
    g_ref = next(it) if has_gain else None
    sc_ref = next(it) if has_scale else None
    tabs = [next(it) for _ in range(n_tabs)]
    o_ref = next(it)

    x = x_ref[...]
    if has_gain:
        xf = x.astype(F32)
        xf = xf * lax.rsqrt(jnp.mean(xf * xf, axis=-1, keepdims=True) + RMS_EPS) * g_ref[...]
        x = xf
    x = x.astype(BF16)
    acc = jnp.dot(x, w_ref[...], preferred_element_type=F32)
    if has_scale:
        acc = acc * sc_ref[...]
    if n_tabs:
        t = [r[...] for r in tabs]
        for j in range(acc.shape[1] // LANES):
            y = acc[:, j * LANES:(j + 1) * LANES]
            out = y * t[0]
            for k, s in enumerate(rope_shifts):
                out = out + pltpu.roll(y, s, 1) * t[k + 1]
            o_ref[:, j * LANES:(j + 1) * LANES] = out.astype(o_ref.dtype)
    else:
        o_ref[...] = acc.astype(o_ref.dtype)


def _mm(x, w, *, out_dtype, gain=None, col_scale=None, rope=None, bm=1024, bn=1024):
    M, K = x.shape
    N = w.shape[1]
    bm = min(bm, M)
    bn = min(bn, N)
    if rope is not None:
        bm = min(bm, rope[1][0].shape[0])
    assert M % bm == 0 and N % bn == 0
    in_specs = [pl.BlockSpec((bm, K), lambda i, j: (i, 0)),
                pl.BlockSpec((K, bn), lambda i, j: (0, j))]
    args = [x, w]
    if gain is not None:
        in_specs.append(pl.BlockSpec((1, K), lambda i, j: (0, 0)))
        args.append(gain.reshape(1, K).astype(F32))
    if col_scale is not None:
        in_specs.append(pl.BlockSpec((1, bn), lambda i, j: (0, j)))
        args.append(col_scale.reshape(1, N).astype(F32))
    shifts, tabs = ((), ())
    if rope is not None:
        shifts, tabs = rope
        seq = tabs[0].shape[0]
        assert seq % bm == 0
        nsb = seq // bm
        for t in tabs:
            in_specs.append(pl.BlockSpec((bm, LANES), lambda i, j: (i % nsb, 0)))
            args.append(t)
    kern = functools.partial(_mm_kernel, has_gain=gain is not None,
                             has_scale=col_scale is not None,
                             rope_shifts=tuple(shifts), n_tabs=len(tabs))
    return pl.pallas_call(
        kern,
        out_shape=jax.ShapeDtypeStruct((M, N), out_dtype),
        grid=(M // bm, N // bn),
        in_specs=in_specs,
        out_specs=pl.BlockSpec((bm, bn), lambda i, j: (i, j)),
        compiler_params=_cparams(("parallel", "parallel")),
        name="mm",
    )(*args)


def _res_ln_kernel(x_ref, y_ref, g_ref, b_ref, o_ref, ob_ref, *, alpha):
    z = alpha * x_ref[...] + y_ref[...]
    mu = jnp.mean(z, axis=-1, keepdims=True)
    zc = z - mu
    var = jnp.mean(zc * zc, axis=-1, keepdims=True)
    out = zc * lax.rsqrt(var + LN_EPS) * g_ref[...] + b_ref[...]
    o_ref[...] = out
    ob_ref[...] = out.astype(BF16)


def _res_ln(x, y, g, b, alpha, bm=512):
    M, D = x.shape
    bm = min(bm, M)
    row = pl.BlockSpec((bm, D), lambda i: (i, 0))
    vec = pl.BlockSpec((1, D), lambda i: (0, 0))
    return pl.pallas_call(
        functools.partial(_res_ln_kernel, alpha=alpha),
        out_shape=(jax.ShapeDtypeStruct((M, D), F32), jax.ShapeDtypeStruct((M, D), BF16)),
        grid=(M // bm,),
        in_specs=[row, row, vec, vec],
        out_specs=(row, row),
        compiler_params=_cparams(("parallel",)),
        name="res_ln",
    )(x, y, g.reshape(1, D), b.reshape(1, D))


SCAN_BLOCK = 256


def _gate_scan_kernel(z_ref, b_ref, o_ref):
    seq = z_ref.shape[0]
    r = lax.broadcasted_iota(jnp.int32, (SCAN_BLOCK, SCAN_BLOCK), 0)
    c = lax.broadcasted_iota(jnp.int32, (SCAN_BLOCK, SCAN_BLOCK), 1)
    tri = jnp.where(c <= r, 1.0, 0.0).astype(F32)
    carry = jnp.zeros((1, LANES), F32)
    for i in range(seq // SCAN_BLOCK):
        z = z_ref[i * SCAN_BLOCK:(i + 1) * SCAN_BLOCK, :] + b_ref[...]
        logf = jnp.minimum(z, 0.0) - jnp.log(1.0 + jnp.exp(-jnp.abs(z)))
        cs = jnp.dot(tri, logf, preferred_element_type=F32,
                     precision=lax.Precision.HIGHEST) + carry
        o_ref[i * SCAN_BLOCK:(i + 1) * SCAN_BLOCK, :] = cs
        carry = cs[SCAN_BLOCK - 1:SCAN_BLOCK, :]


def _gate_scan(z, bias_row, batch, seq):
    assert seq % SCAN_BLOCK == 0
    return pl.pallas_call(
        _gate_scan_kernel,
        out_shape=jax.ShapeDtypeStruct((batch * seq, LANES), F32),
        grid=(batch,),
        in_specs=[pl.BlockSpec((seq, LANES), lambda b: (b, 0)),
                  pl.BlockSpec((1, LANES), lambda b: (0, 0))],
        out_specs=pl.BlockSpec((seq, LANES), lambda b: (b, 0)),
        compiler_params=_cparams(("parallel",)),
        name="gate_scan",
    )(z, bias_row)


def _attn_kernel(*refs, mode, bq, bk, heads_per_step, dv, v_group, rsub, has_q2, has_decay, has_bias):
    qt, kt, ft, lt, mt = refs[:5]
    it = iter(refs[5:])
    q_ref = next(it)
    k_ref = next(it)
    v_ref = next(it)
    q2_ref = next(it) if has_q2 else None
    k2_ref = next(it) if has_q2 else None
    cq_ref = next(it) if has_decay else None
    ck_ref = next(it) if has_decay else None
    bias_ref = next(it) if has_bias else None
    o_ref = next(it)
    m_sc = next(it)
    l_sc = next(it)
    acc_sc = next(it)

    p = pl.program_id(2)
    qi = qt[p]
    ki = kt[p]

    @pl.when(ft[p] == 1)
    def _():
        m_sc[...] = jnp.full(m_sc.shape, NEG_INF, F32)
        l_sc[...] = jnp.zeros(l_sc.shape, F32)
        acc_sc[...] = jnp.zeros(acc_sc.shape, F32)

    nt = (((1,), (1,)), ((), ()))

    def step(masked):
        for g in range(heads_per_step):
            hs = slice(g * HEAD_DIM, (g + 1) * HEAD_DIM)
            vs = slice((g // v_group) * dv, (g // v_group + 1) * dv)
            for r0 in range(0, bq, rsub):
                rs = slice(r0, r0 + rsub)
                s = lax.dot_general(q_ref[rs, hs], k_ref[:, hs], nt, preferred_element_type=F32)
                if has_q2:
                    s = s + lax.dot_general(q2_ref[rs, hs], k2_ref[...], nt,
                                            preferred_element_type=F32)
                if has_decay:
                    s = s + (cq_ref[g, rs, :] - ck_ref[g])
                if has_bias:
                    s = s + bias_ref[g, 0, rs, :]
                if masked:
                    rows = qi * bq + r0 + lax.broadcasted_iota(jnp.int32, (rsub, bk), 0)
                    cols = ki * bk + lax.broadcasted_iota(jnp.int32, (rsub, bk), 1)
                    if mode == "causal":
                        allowed = cols <= rows
                    else:
                        qc = lax.shift_right_logical(rows, 6)
                        kc = lax.shift_right_logical(cols, 6)
                        if mode == "chunk":
                            allowed = kc <= qc
                        else:
                            dist = qc - kc
                            allowed = dist * (LEFT_CHUNKS - dist) >= 0
                    s = jnp.where(allowed, s, NEG_INF)
                m_prev = m_sc[g, rs, :]
                m_new = jnp.maximum(m_prev, jnp.max(s, axis=1, keepdims=True))
                alpha = jnp.exp2(m_prev - m_new)
                pr = jnp.exp2(s - m_new)
                l_sc[g, rs, :] = alpha * l_sc[g, rs, :] + jnp.sum(pr, axis=1, keepdims=True)
                acc_sc[g, rs, :] = alpha * acc_sc[g, rs, :] + jnp.dot(
                    pr.astype(BF16), v_ref[:, vs], preferred_element_type=F32)
                m_sc[g, rs, :] = m_new

    @pl.when(mt[p] == 1)
    def _():
        step(True)

    @pl.when(mt[p] == 0)
    def _():
        step(False)

    @pl.when(lt[p] == 1)
    def _():
        for g in range(heads_per_step):
            o_ref[:, g * dv:(g + 1) * dv] = (acc_sc[g] / l_sc[g]).astype(o_ref.dtype)


def _pair_tables(nq, mode, bq, bk):
    qs, ks, ms = [], [], []
    for qi in range(nq):
        if mode == "band":
            lo = max(0, (qi * bq - LEFT_CHUNKS * CHUNK) // bk)
        else:
            lo = 0
        hi = ((qi + 1) * bq - 1) // bk
        for ki in range(lo, hi + 1):
            qs.append(qi)
            ks.append(ki)
            q_lo, q_hi = qi * bq, (qi + 1) * bq - 1
            k_lo, k_hi = ki * bk, (ki + 1) * bk - 1
            if mode == "causal":
                full = k_hi <= q_lo
            elif mode == "chunk":
                full = k_hi // CHUNK <= q_lo // CHUNK
            else:
                full = (k_hi // CHUNK <= q_lo // CHUNK
                        and k_lo // CHUNK + LEFT_CHUNKS >= q_hi // CHUNK)
            ms.append(0 if full else 1)
    n = len(qs)
    first = [1 if (i == 0 or qs[i - 1] != qs[i]) else 0 for i in range(n)]
    last = [1 if (i == n - 1 or qs[i + 1] != qs[i]) else 0 for i in range(n)]
    as_i32 = lambda a: jnp.asarray(np.array(a, dtype=np.int32))
    return as_i32(qs), as_i32(ks), as_i32(first), as_i32(last), as_i32(ms), n


def _attention(q, k, v, *, batch, seq, heads, q_off, k_off, v_off, dv, v_group, mode,
               bq, bk, out_dtype, heads_per_step=2, rsub=256,
               q2=None, k2=None, q2_off=0, cq=None, ck=None, bias=None):
    bq = min(bq, seq)
    bk = min(bk, seq)
    rsub = min(rsub, bq)
    nq = seq // bq
    nk = seq // bk
    G = heads_per_step
    assert heads % G == 0 and G % v_group == 0 and q_off % G == 0 and k_off % G == 0
    assert (v_off * v_group) % G == 0 and q2_off % G == 0
    vw = dv * G // v_group
    qt, kt, ft, lt, mt, npairs = _pair_tables(nq, mode, bq, bk)
    has_q2 = q2 is not None
    has_decay = cq is not None
    has_bias = bias is not None
    W = G * HEAD_DIM

    def im(off_blocks, row_tab, nblk):
        def f(b, h, p, qt, kt, ft, lt, mt):
            t = qt if row_tab == "q" else kt
            return (b * nblk + t[p], off_blocks + h)
        return f

    in_specs = [
        pl.BlockSpec((bq, W), im(q_off // G, "q", nq)),
        pl.BlockSpec((bk, W), im(k_off // G, "k", nk)),
        pl.BlockSpec((bk, vw), im(v_off * v_group // G, "k", nk)),
    ]
    args = [q, k, v]
    if has_q2:
        in_specs += [
            pl.BlockSpec((bq, W), im(q2_off // G, "q", nq)),
            pl.BlockSpec((bk, HEAD_DIM), lambda b, h, p, qt, kt, ft, lt, mt: (b * nk + kt[p], 0)),
        ]
        args += [q2, k2]
    if has_decay:
        hb = heads // G
        in_specs += [
            pl.BlockSpec((G, bq, 1), lambda b, h, p, qt, kt, ft, lt, mt: (b * hb + h, qt[p], 0)),
            pl.BlockSpec((G, 1, bk), lambda b, h, p, qt, kt, ft, lt, mt: (b * hb + h, 0, kt[p])),
        ]
        args += [cq, ck]
    if has_bias:
        in_specs.append(
            pl.BlockSpec((G, 1, bq, bk), lambda b, h, p, qt, kt, ft, lt, mt: (h, qt[p] - kt[p], 0, 0)))
        args.append(bias)

    kern = functools.partial(_attn_kernel, mode=mode, bq=bq, bk=bk, heads_per_step=G, dv=dv,
                             v_group=v_group, rsub=rsub, has_q2=has_q2, has_decay=has_decay,
                             has_bias=has_bias)
    return pl.pallas_call(
        kern,
        out_shape=jax.ShapeDtypeStruct((batch * seq, heads * dv), out_dtype),
        grid_spec=pltpu.PrefetchScalarGridSpec(
            num_scalar_prefetch=5,
            grid=(batch, heads // G, npairs),
            in_specs=in_specs,
            out_specs=pl.BlockSpec((bq, G * dv),
                                   lambda b, h, p, qt, kt, ft, lt, mt: (b * nq + qt[p], h)),
            scratch_shapes=[pltpu.VMEM((G, bq, 1), F32), pltpu.VMEM((G, bq, 1), F32),
                            pltpu.VMEM((G, bq, dv), F32)],
        ),
        compiler_params=_cparams(("parallel", "parallel", "arbitrary")),
        name="attn_" + mode,
    )(qt, kt, ft, lt, mt, *args)


def _diff_combine_kernel(a_ref, lam_ref, g_ref, o_ref, *, lam_init):
    lf = lam_ref[...]
    lam = (jnp.exp(jnp.sum(lf[0:1] * lf[1:2], axis=-1, keepdims=True))
           - jnp.exp(jnp.sum(lf[2:3] * lf[3:4], axis=-1, keepdims=True)) + lam_init)
    w = 2 * DIFF_DIM
    for hd in range(H_DIFF):
        a1 = a_ref[:, (2 * hd) * w:(2 * hd + 1) * w]
        a2 = a_ref[:, (2 * hd + 1) * w:(2 * hd + 2) * w]
        d = a1 - lam * a2
        d = d * lax.rsqrt(jnp.mean(d * d, axis=-1, keepdims=True) + RMS_EPS) * g_ref[...]
        o_ref[:, hd * w:(hd + 1) * w] = (d * (1.0 - lam_init)).astype(o_ref.dtype)


def _diff_combine(a, diff_lambda, g_subln, lam_init, bm=512):
    M = a.shape[0]
    bm = min(bm, M)
    w = 2 * DIFF_DIM
    return pl.pallas_call(
        functools.partial(_diff_combine_kernel, lam_init=lam_init),
        out_shape=jax.ShapeDtypeStruct((M, H_DIFF * w), BF16),
        grid=(M // bm,),
        in_specs=[pl.BlockSpec((bm, 2 * H_DIFF * w), lambda i: (i, 0)),
                  pl.BlockSpec((4, DIFF_DIM), lambda i: (0, 0)),
                  pl.BlockSpec((1, w), lambda i: (0, 0))],
        out_specs=pl.BlockSpec((bm, H_DIFF * w), lambda i: (i, 0)),
        compiler_params=_cparams(("parallel",)),
        name="diff_combine",
    )(a, diff_lambda.astype(F32), g_subln.reshape(1, w).astype(F32))


PEER_SUB = 256


def _top_values(s, n):
    vals = []
    work = s
    for r in range(n):
        m = jnp.max(work, axis=0, keepdims=True)
        vals.append(m)
        if r + 1 < n:
            work = jnp.where(work >= m, -jnp.inf, work)
    return vals


def _peer_stats_kernel(q_ref, keys_ref, s2_ref, c_ref, tau_ref):
    nt = (((1,), (1,)), ((), ()))
    tb = q_ref.shape[0]
    k1 = keys_ref[0, 0]
    k2 = keys_ref[0, 1]
    for t0 in range(0, tb, PEER_SUB):
        q1 = q_ref[t0:t0 + PEER_SUB, 0:PEER_HALF]
        q2 = q_ref[t0:t0 + PEER_SUB, PEER_HALF:2 * PEER_HALF]
        s1 = lax.dot_general(k1, q1, nt, preferred_element_type=F32)
        s2 = lax.dot_general(k2, q2, nt, preferred_element_type=F32)
        sv1 = _top_values(s1, PEER_TOPK)
        sv2 = jnp.concatenate(_top_values(s2, PEER_TOPK), axis=0)
        cand = jnp.concatenate([sv1[a] + sv2 for a in range(PEER_TOPK)], axis=0)
        top = _top_values(cand, PEER_TOPK + 1)
        cmax = top[0]
        tau = 0.5 * (top[PEER_TOPK - 1] + top[PEER_TOPK])
        z = jnp.sum(jnp.where(cand >= tau, jnp.exp(cand - cmax), 0.0), axis=0, keepdims=True)
        lse = cmax + jnp.log(z)
        s2_ref[0, :, t0:t0 + PEER_SUB] = s2 * LOG2E
        c_ref[0, :, t0:t0 + PEER_SUB] = (s1 - lse) * LOG2E
        tau_ref[0, :, t0:t0 + PEER_SUB] = (tau - lse) * LOG2E


def _peer_stats(q, keys, tb=1024):
    T = q.shape[0]
    tb = min(tb, T)
    assert T % tb == 0 and tb % PEER_SUB == 0
    big = jax.ShapeDtypeStruct((PEER_HEADS, N_KEYS, T), F32)
    return pl.pallas_call(
        _peer_stats_kernel,
        out_shape=(big, big, jax.ShapeDtypeStruct((PEER_HEADS, 1, T), F32)),
        grid=(T // tb, PEER_HEADS),
        in_specs=[pl.BlockSpec((tb, 2 * PEER_HALF), lambda i, h: (i, h)),
                  pl.BlockSpec((1, 2, N_KEYS, PEER_HALF), lambda i, h: (h, 0, 0, 0))],
        out_specs=(pl.BlockSpec((1, N_KEYS, tb), lambda i, h: (h, 0, i)),
                   pl.BlockSpec((1, N_KEYS, tb), lambda i, h: (h, 0, i)),
                   pl.BlockSpec((1, 1, tb), lambda i, h: (h, 0, i))),
        compiler_params=_cparams(("parallel", "parallel")),
        name="peer_stats",
    )(q, keys)


GELU_K = math.sqrt(2.0 / math.pi)
GATE_ROWS = 32
MIX_TOK = 128


def _peer_mix_body(xT_ref, u_ref, vT_ref, s2_ref, c_ref, tau_ref, o_ref,
                   h_wr, h_rd, w_wr, w_rd, ni):
    tb = xT_ref.shape[1]
    d = vT_ref.shape[0]
    c_rows = d // ni

    def gate_unit(il, t0):
        ts = slice(t0, t0 + MIX_TOK)
        for k0 in range(0, N_KEYS, GATE_ROWS):
            ks = slice(k0, k0 + GATE_ROWS)
            rows = slice(il * N_KEYS + k0, il * N_KEYS + k0 + GATE_ROWS)
            g = None
            for h in range(PEER_HEADS):
                z = s2_ref[h, ks, ts] + c_ref[h, 0, il:il + 1, ts]
                e = jnp.where(z >= tau_ref[h, :, ts], jnp.exp2(z), 0.0)
                g = e if g is None else g + e
            x = h_rd[rows, ts]
            inner = x * ((x * x) * (0.044715 * GELU_K) + GELU_K)
            hx = 0.5 * x
            act = hx + hx * jnp.tanh(inner)
            w_wr[rows, ts] = (g * act).astype(BF16)

    for t0 in range(0, tb, 2 * MIX_TOK):
        tq = slice(t0, t0 + 2 * MIX_TOK)
        for r in range(ni):
            h_wr[r * N_KEYS:(r + 1) * N_KEYS, tq] = jnp.dot(
                u_ref[r * N_KEYS:(r + 1) * N_KEYS, :], xT_ref[:, tq], preferred_element_type=F32)
            gate_unit(r, t0)
            gate_unit(r, t0 + MIX_TOK)
            o_ref[r * c_rows:(r + 1) * c_rows, tq] += jnp.dot(
                vT_ref[r * c_rows:(r + 1) * c_rows, :], w_rd[:, tq], preferred_element_type=F32)


def _peer_mix_kernel(xT_ref, u_ref, vT_ref, s2_ref, c_ref, tau_ref, o_ref, h0, h1, w0, w1, *, ni):
    c = pl.program_id(1)

    @pl.when(c == 0)
    def _():
        o_ref[...] = jnp.zeros(o_ref.shape, F32)
        h1[...] = jnp.zeros(h1.shape, F32)
        w0[...] = jnp.zeros(w0.shape, BF16)

    args = (xT_ref, u_ref, vT_ref, s2_ref, c_ref, tau_ref, o_ref)

    @pl.when(lax.rem(c, 2) == 0)
    def _():
        _peer_mix_body(*args, h0, h1, w1, w0, ni)

    @pl.when(lax.rem(c, 2) == 1)
    def _():
        _peer_mix_body(*args, h1, h0, w0, w1, ni)


def _peer_mix(xT, u, vT, s2, c, tau, *, tb=512, ec=512):
    D, T = xT.shape
    E = u.shape[0]
    tb = min(tb, T)
    assert T % tb == 0 and E % ec == 0 and ec % N_KEYS == 0 and tb % (2 * MIX_TOK) == 0
    ni = ec // N_KEYS
    nc = E // ec
    once = pl.Buffered(1)
    return pl.pallas_call(
        functools.partial(_peer_mix_kernel, ni=ni),
        out_shape=jax.ShapeDtypeStruct((D, T), F32),
        grid=(T // tb, nc + 2),
        in_specs=[pl.BlockSpec((D, tb), lambda i, e: (0, i), pipeline_mode=once),
                  pl.BlockSpec((ec, D), lambda i, e: (jnp.minimum(e, nc - 1), 0)),
                  pl.BlockSpec((D, ec), lambda i, e: (0, jnp.clip(e - 2, 0, nc - 1))),
                  pl.BlockSpec((PEER_HEADS, N_KEYS, tb), lambda i, e: (0, 0, i), pipeline_mode=once),
                  pl.BlockSpec((PEER_HEADS, 1, ni, tb), lambda i, e: (0, jnp.clip(e - 1, 0, nc - 1), 0, i)),
                  pl.BlockSpec((PEER_HEADS, 1, tb), lambda i, e: (0, 0, i), pipeline_mode=once)],
        out_specs=pl.BlockSpec((D, tb), lambda i, e: (0, i)),
        scratch_shapes=[pltpu.VMEM((ec, tb), F32), pltpu.VMEM((ec, tb), F32),
                        pltpu.VMEM((ec, tb), BF16), pltpu.VMEM((ec, tb), BF16)],
        compiler_params=_cparams(("parallel", "arbitrary")),
        name="peer_mix",
    )(xT, u, vT, s2, c.reshape(PEER_HEADS, nc, ni, T), tau)


def _rope_tables(seq, d):
    inv = ROPE_THETA ** (-jnp.arange(0, d, 2, dtype=F32) / d)
    ang = jnp.arange(seq, dtype=F32)[:, None] * inv[None, :]
    cos, sin = jnp.cos(ang), jnp.sin(ang)
    half = d // 2
    pad = jnp.zeros((seq, LANES - d), F32)
    t0 = jnp.concatenate([cos, cos, pad], axis=1)
    if d == LANES:
        return (half,), (t0, jnp.concatenate([-sin, sin], axis=1))
    zh = jnp.zeros((seq, half), F32)
    t1 = jnp.concatenate([-sin, zh, pad], axis=1)
    t2 = jnp.concatenate([zh, sin, pad], axis=1)
    return (LANES - half, half), (t0, t1, t2)


def _band_bias_tiles(rel_bias, bq, bk):
    assert bq == bk
    n = bq
    nd = (LEFT_CHUNKS * CHUNK + bq - 1) // bk + 1
    m = np.arange(2 * n)
    tiles = []
    for dlt in range(nd):
        rel = np.where(m < n, dlt * n - m, dlt * n + 2 * n - m)
        ridx = np.clip(rel, -REL_MAX, REL_MAX) + REL_MAX
        pat = jnp.take(rel_bias.astype(F32), jnp.asarray(ridx), axis=1)
        nh = pat.shape[0]
        flat = jnp.broadcast_to(pat[:, None, :], (nh, n, 2 * n)).reshape(nh, n * 2 * n)
        tiles.append(flat[:, :n * (2 * n - 1)].reshape(nh, n, 2 * n - 1)[:, :, :n])
    return jnp.stack(tiles, axis=1)


def _even_layer(xb, w_in, b_forget, rel_bias, w_out, batch, seq):
    hd = HEAD_DIM
    scale = hd ** -0.5 * LOG2E
    n = H_FOX * hd
    fq, fk, fv = w_in[:, 0:n], w_in[:, n:2 * n], w_in[:, 2 * n:3 * n]
    wf = w_in[:, 3 * n:3 * n + H_FOX]
    o = 3 * n + H_FOX
    m = H_CHK * hd
    cq, ck, cv = w_in[:, o:o + m], w_in[:, o + m:o + 2 * m], w_in[:, o + 2 * m:o + 3 * m]
    w_main = jnp.concatenate([fq, cq, fk, fv, ck, cv], axis=1).astype(BF16)
    col_scale = jnp.concatenate([jnp.full((n + m,), scale, F32), jnp.ones((2 * n + 2 * m,), F32)])
    proj = _mm(xb, w_main, out_dtype=BF16, col_scale=col_scale)

    wf_pad = jnp.pad(wf, ((0, 0), (0, LANES - H_FOX))).astype(BF16)
    logits = _mm(xb, wf_pad, out_dtype=F32)
    bias_row = jnp.pad(b_forget.astype(F32), (0, LANES - H_FOX)).reshape(1, LANES)
    cum = _gate_scan(logits, bias_row, batch, seq)
    cum = (cum[:, :H_FOX] * LOG2E).reshape(batch, seq, H_FOX).transpose(0, 2, 1)
    cq_col = cum.reshape(batch * H_FOX, seq, 1)
    ck_row = cum.reshape(batch * H_FOX, 1, seq)

    nh = n // hd
    o_fox = _attention(proj, proj, proj, batch=batch, seq=seq, heads=H_FOX,
                       q_off=0, k_off=2 * nh, v_off=3 * nh, dv=hd, v_group=1, mode="causal",
                       bq=512, bk=512, out_dtype=BF16, cq=cq_col, ck=ck_row)
    bqb = min(256, seq)
    bias = _band_bias_tiles(rel_bias * LOG2E, bqb, bqb)
    o_chk = _attention(proj, proj, proj, batch=batch, seq=seq, heads=H_CHK,
                       q_off=nh, k_off=4 * nh, v_off=5 * nh, dv=hd, v_group=1, mode="band",
                       bq=bqb, bk=bqb, out_dtype=BF16, bias=bias)
    o_cat = jnp.concatenate([o_fox, o_chk], axis=1)
    return _mm(o_cat, w_out.astype(BF16), out_dtype=F32)


def _odd_layer(xb, w_in, g_q_lora, g_kv_lora, w_uq, w_ukv, diff_lambda, g_subln, w_out,
               layer_idx, batch, seq):
    o1 = Q_LORA + KV_LORA
    o2 = o1 + MLA_ROPE
    nd = H_DIFF * 2 * DIFF_DIM
    w_lora = w_in[:, :o1].astype(BF16)
    w_kpe = jnp.pad(w_in[:, o1:o2], ((0, 0), (0, LANES - MLA_ROPE))).astype(BF16)
    w_dqk = w_in[:, o2:o2 + 2 * nd].astype(BF16)
    w_dv = w_in[:, o2 + 2 * nd:o2 + 3 * nd].astype(BF16)

    rope128 = _rope_tables(seq, DIFF_DIM)
    rope64 = _rope_tables(seq, MLA_ROPE)

    lora = _mm(xb, w_lora, out_dtype=F32)
    kpe = _mm(xb, w_kpe, out_dtype=BF16, rope=rope64)
    dscale = jnp.concatenate([jnp.full((nd,), DIFF_DIM ** -0.5 * LOG2E, F32), jnp.ones((nd,), F32)])
    dqk = _mm(xb, w_dqk, out_dtype=BF16, col_scale=dscale, rope=rope128)
    dv = _mm(xb, w_dv, out_dtype=BF16)

    mscale = (MLA_NOPE + MLA_ROPE) ** -0.5 * LOG2E
    wq = w_uq.reshape(Q_LORA, H_MLA, MLA_NOPE + MLA_ROPE)
    wq_nope = wq[:, :, :MLA_NOPE].reshape(Q_LORA, H_MLA * MLA_NOPE).astype(BF16)
    wq_rope = jnp.pad(wq[:, :, MLA_NOPE:], ((0, 0), (0, 0), (0, LANES - MLA_ROPE)))
    wq_rope = wq_rope.reshape(Q_LORA, H_MLA * LANES).astype(BF16)
    cq = lora[:, :Q_LORA]
    ckv = lora[:, Q_LORA:]
    ms = jnp.full((H_MLA * LANES,), mscale, F32)
    q_nope = _mm(cq, wq_nope, out_dtype=BF16, gain=g_q_lora, col_scale=ms)
    q_rope = _mm(cq, wq_rope, out_dtype=BF16, gain=g_q_lora, col_scale=ms, rope=rope64)
    wkv = w_ukv.reshape(KV_LORA, H_MLA, MLA_NOPE + MLA_V)
    wkv = jnp.concatenate([wkv[:, :, :MLA_NOPE].reshape(KV_LORA, H_MLA * MLA_NOPE),
                           wkv[:, :, MLA_NOPE:].reshape(KV_LORA, H_MLA * MLA_V)], axis=1)
    kv = _mm(ckv, wkv.astype(BF16), out_dtype=BF16, gain=g_kv_lora)

    o_mla = _attention(q_nope, kv, kv, batch=batch, seq=seq, heads=H_MLA,
                       q_off=0, k_off=0, v_off=H_MLA, dv=MLA_V, v_group=1, mode="chunk",
                       bq=512, bk=512, out_dtype=BF16, q2=q_rope, k2=kpe, q2_off=0)
    a = _attention(dqk, dqk, dv, batch=batch, seq=seq, heads=2 * H_DIFF,
                   q_off=0, k_off=2 * H_DIFF, v_off=0, dv=2 * DIFF_DIM, v_group=2, mode="chunk",
                   bq=512, bk=512, out_dtype=F32)
    lam_init = 0.8 - 0.6 * math.exp(-0.3 * layer_idx)
    o_diff = _diff_combine(a, diff_lambda, g_subln, lam_init)
    o_cat = jnp.concatenate([o_mla, o_diff], axis=1)
    return _mm(o_cat, w_out.astype(BF16), out_dtype=F32)


def _peer_layer(xb, w_query, sub_keys, u_tab, v_tab):
    q = _mm(xb, w_query.astype(BF16), out_dtype=BF16)
    s2, c, tau = _peer_stats(q, sub_keys.astype(BF16))
    yT = _peer_mix(xb.T, u_tab.astype(BF16), v_tab.T.astype(BF16), s2, c, tau)
    return yT.T


def kernel(x, w_in_even, b_forget, rel_bias, w_out_even, w_in_odd, g_q_lora, g_kv_lora, w_uq, w_ukv, diff_lambda, g_subln, w_out_odd, peer_w_query, peer_sub_keys, peer_u, peer_v, ln_mix_g, ln_mix_b, ln_ffn_g, ln_ffn_b):
    batch, seq, d = x.shape
    depth = ln_mix_g.shape[0]
    alpha = (2 * depth) ** 0.25
    xf = x.reshape(batch * seq, d).astype(F32)
    xb = xf.astype(BF16)
    for l in range(depth):
        i = l // 2
        if l % 2 == 0:
            y = _even_layer(xb, w_in_even[i], b_forget[i], rel_bias[i], w_out_even[i], batch, seq)
        else:
            y = _odd_layer(xb, w_in_odd[i], g_q_lora[i], g_kv_lora[i], w_uq[i], w_ukv[i],
                           diff_lambda[i], g_subln[i], w_out_odd[i], l, batch, seq)
        xf, xb = _res_ln(xf, y, ln_mix_g[l], ln_mix_b[l], alpha)
        y = _peer_layer(xb, peer_w_query[l], peer_sub_keys[l], peer_u[l], peer_v[l])
        xf, xb = _res_ln(xf, y, ln_ffn_g[l], ln_ffn_b[l], alpha)
    return xf.reshape(batch, seq, d)
```

```python
import functools
import math

import numpy as np
import jax
import jax.numpy as jnp
from jax import lax
from jax.experimental import pallas as pl
from jax.experimental.pallas import tpu as pltpu

F32 = jnp.float32
BF16 = jnp.bfloat16

CHUNK = 64
HEAD_DIM = 128
ROPE_THETA = 10000.0
LN_EPS = 1e-5
RMS_EPS = 1e-6
NEG_INF = -1e30
H_FOX = 8
H_CHK = 8
LEFT_CHUNKS = 8
REL_MAX = 128
H_MLA = 8
Q_LORA = 512
KV_LORA = 512
MLA_NOPE = 128
MLA_ROPE = 64
MLA_V = 128
H_DIFF = 4
DIFF_DIM = 128
N_KEYS = 128
PEER_HEADS = 8
PEER_TOPK = 16
PEER_HALF = 128

LOG2E = math.log2(math.e)
LANES = 128
VMEM_LIMIT = 56 * 1024 * 1024


def _cparams(sem):
    return pltpu.CompilerParams(dimension_semantics=sem, vmem_limit_bytes=VMEM_LIMIT)


def _mm_kernel(*refs, has_gain, has_scale, rope_shifts, n_tabs):
    it = iter(refs)
    x_ref = next(it)
    w_ref = next(it)
    g_ref = next(it) if has_gain else None
    sc_ref = next(it) if has_scale else None
    tabs = [next(it) for _ in range(n_tabs)]
    o_ref = next(it)

    x = x_ref[...]
    if has_gain:
        xf = x.astype(F32)
        xf = xf * lax.rsqrt(jnp.mean(xf * xf, axis=-1, keepdims=True) + RMS_EPS) * g_ref[...]
        x = xf
    x = x.astype(BF16)
    acc = jnp.dot(x, w_ref[...], preferred_element_type=F32)
    if has_scale:
        acc = acc * sc_ref[...]
    if n_tabs:
        t = [r[...] for r in tabs]
        for j in range(acc.shape[1] // LANES):
            y = acc[:, j * LANES:(j + 1) * LANES]
            out = y * t[0]
            for k, s in enumerate(rope_shifts):
                out = out + pltpu.roll(y, s, 1) * t[k + 1]
            o_ref[:, j * LANES:(j + 1) * LANES] = out.astype(o_ref.dtype)
    else:
        o_ref[...] = acc.astype(o_ref.dtype)


def _mm(x, w, *, out_dtype, gain=None, col_scale=None, rope=None, bm=1024, bn=1024):
    M, K = x.shape
    N = w.shape[1]
    bm = min(bm, M)
    bn = min(bn, N)
    if rope is not None:
        bm = min(bm, rope[1][0].shape[0])
    assert M % bm == 0 and N % bn == 0
    in_specs = [pl.BlockSpec((bm, K), lambda i, j: (i, 0)),
                pl.BlockSpec((K, bn), lambda i, j: (0, j))]
    args = [x, w]
    if gain is not None:
        in_specs.append(pl.BlockSpec((1, K), lambda i, j: (0, 0)))
        args.append(gain.reshape(1, K).astype(F32))
    if col_scale is not None:
        in_specs.append(pl.BlockSpec((1, bn), lambda i, j: (0, j)))
        args.append(col_scale.reshape(1, N).astype(F32))
    shifts, tabs = ((), ())
    if rope is not None:
        shifts, tabs = rope
        seq = tabs[0].shape[0]
        assert seq % bm == 0
        nsb = seq // bm
        for t in tabs:
            in_specs.append(pl.BlockSpec((bm, LANES), lambda i, j: (i % nsb, 0)))
            args.append(t)
    kern = functools.partial(_mm_kernel, has_gain=gain is not None,
                             has_scale=col_scale is not None,
                             rope_shifts=tuple(shifts), n_tabs=len(tabs))
    return pl.pallas_call(
        kern,
        out_shape=jax.ShapeDtypeStruct((M, N), out_dtype),
        grid=(M // bm, N // bn),
        in_specs=in_specs,
        out_specs=pl.BlockSpec((bm, bn), lambda i, j: (i, j)),
        compiler_params=_cparams(("parallel", "parallel")),
        name="mm",
    )(*args)


def _res_ln_kernel(x_ref, y_ref, g_ref, b_ref, o_ref, ob_ref, obt_ref, *, alpha, y_transposed):
    y = y_ref[...]
    if y_transposed:
        y = y.T
    z = alpha * x_ref[...] + y
    mu = jnp.mean(z, axis=-1, keepdims=True)
    zc = z - mu
    var = jnp.mean(zc * zc, axis=-1, keepdims=True)
    out = zc * lax.rsqrt(var + LN_EPS) * g_ref[...] + b_ref[...]
    o_ref[...] = out
    ob_ref[...] = out.astype(BF16)
    if obt_ref is not None:
        obt_ref[...] = out.T.astype(BF16)


def _res_ln(x, y, g, b, alpha, *, y_transposed=False, emit_transposed=False, bm=512):
    M, D = x.shape
    bm = min(bm, M)
    row = pl.BlockSpec((bm, D), lambda i: (i, 0))
    col = pl.BlockSpec((D, bm), lambda i: (0, i))
    vec = pl.BlockSpec((1, D), lambda i: (0, 0))
    out_shape = [jax.ShapeDtypeStruct((M, D), F32), jax.ShapeDtypeStruct((M, D), BF16)]
    out_specs = [row, row]
    if emit_transposed:
        out_shape.append(jax.ShapeDtypeStruct((D, M), BF16))
        out_specs.append(col)
        kern = functools.partial(_res_ln_kernel, alpha=alpha, y_transposed=y_transposed)
    else:
        kern = functools.partial(_res_ln_kernel, obt_ref=None, alpha=alpha, y_transposed=y_transposed)
    return pl.pallas_call(
        kern,
        out_shape=tuple(out_shape),
        grid=(M // bm,),
        in_specs=[row, col if y_transposed else row, vec, vec],
        out_specs=tuple(out_specs),
        compiler_params=_cparams(("parallel",)),
        name="res_ln",
    )(x, y, g.reshape(1, D), b.reshape(1, D))


SCAN_BLOCK = 256


def _gate_scan_kernel(z_ref, b_ref, o_ref):
    seq = z_ref.shape[0]
    r = lax.broadcasted_iota(jnp.int32, (SCAN_BLOCK, SCAN_BLOCK), 0)
    c = lax.broadcasted_iota(jnp.int32, (SCAN_BLOCK, SCAN_BLOCK), 1)
    tri = jnp.where(c <= r, 1.0, 0.0).astype(F32)
    carry = jnp.zeros((1, LANES), F32)
    for i in range(seq // SCAN_BLOCK):
        z = z_ref[i * SCAN_BLOCK:(i + 1) * SCAN_BLOCK, :] + b_ref[...]
        logf = jnp.minimum(z, 0.0) - jnp.log(1.0 + jnp.exp(-jnp.abs(z)))
        cs = jnp.dot(tri, logf, preferred_element_type=F32,
                     precision=lax.Precision.HIGHEST) + carry
        o_ref[i * SCAN_BLOCK:(i + 1) * SCAN_BLOCK, :] = cs
        carry = cs[SCAN_BLOCK - 1:SCAN_BLOCK, :]


def _gate_scan(z, bias_row, batch, seq):
    assert seq % SCAN_BLOCK == 0
    return pl.pallas_call(
        _gate_scan_kernel,
        out_shape=jax.ShapeDtypeStruct((batch * seq, LANES), F32),
        grid=(batch,),
        in_specs=[pl.BlockSpec((seq, LANES), lambda b: (b, 0)),
                  pl.BlockSpec((1, LANES), lambda b: (0, 0))],
        out_specs=pl.BlockSpec((seq, LANES), lambda b: (b, 0)),
        compiler_params=_cparams(("parallel",)),
        name="gate_scan",
    )(z, bias_row)


def _attn_kernel(*refs, mode, bq, bk, heads_per_step, dv, v_group, rsub, has_x, x_shared,
                 has_bias, ones_col):
    qt, kt, ft, lt, mt = refs[:5]
    it = iter(refs[5:])
    q_ref = next(it)
    k_ref = next(it)
    v_ref = next(it)
    qx_ref = next(it) if has_x else None
    kx_ref = next(it) if has_x else None
    bias_ref = next(it) if has_bias else None
    o_ref = next(it)
    m_sc = next(it)
    acc_sc = next(it)
    l_sc = None if ones_col else next(it)

    p = pl.program_id(2)
    qi = qt[p]
    ki = kt[p]
    accw = acc_sc.shape[2]

    @pl.when(ft[p] == 1)
    def _():
        m_sc[...] = jnp.full(m_sc.shape, NEG_INF, F32)
        acc_sc[...] = jnp.zeros(acc_sc.shape, F32)
        if l_sc is not None:
            l_sc[...] = jnp.zeros(l_sc.shape, F32)

    nt = (((1,), (1,)), ((), ()))

    def lanes(x, width):
        return x if width == LANES else jnp.concatenate([x] * (width // LANES), axis=1)

    def step(masked):
        for g in range(heads_per_step):
            hs = slice(g * HEAD_DIM, (g + 1) * HEAD_DIM)
            vs = slice((g // v_group) * dv, (g // v_group + 1) * dv)
            kk = k_ref[:, hs]
            if has_x:
                kk = jnp.concatenate([kk, kx_ref[...] if x_shared else kx_ref[:, hs]], axis=1)
            vv = v_ref[:, vs]
            if ones_col:
                vv = jnp.concatenate([vv, jnp.ones((bk, LANES), BF16)], axis=1)
            for r0 in range(0, bq, rsub):
                rs = slice(r0, r0 + rsub)
                qq = q_ref[rs, hs]
                if has_x:
                    qq = jnp.concatenate([qq, qx_ref[rs, hs]], axis=1)
                s = lax.dot_general(qq, kk, nt, preferred_element_type=F32)
                if has_bias:
                    s = s + bias_ref[g, 0, rs, :]
                if masked:
                    rows = qi * bq + r0 + lax.broadcasted_iota(jnp.int32, (rsub, bk), 0)
                    cols = ki * bk + lax.broadcasted_iota(jnp.int32, (rsub, bk), 1)
                    if mode == "causal":
                        allowed = cols <= rows
                    else:
                        qc = lax.shift_right_logical(rows, 6)
                        kc = lax.shift_right_logical(cols, 6)
                        if mode == "chunk":
                            allowed = kc <= qc
                        else:
                            dist = qc - kc
                            allowed = dist * (LEFT_CHUNKS - dist) >= 0
                    s = jnp.where(allowed, s, NEG_INF)
                m_prev = m_sc[g, rs, :]
                m_new = jnp.maximum(m_prev, jnp.max(s, axis=1, keepdims=True))
                alpha = jnp.exp2(m_prev - m_new)
                pr = jnp.exp2(s - lanes(m_new, bk))
                pv = jnp.dot(pr.astype(BF16), vv, preferred_element_type=F32)
                acc_sc[g, rs, :] = lanes(alpha, accw) * acc_sc[g, rs, :] + pv
                if l_sc is not None:
                    l_sc[g, rs, :] = alpha * l_sc[g, rs, :] + jnp.sum(pr, axis=1, keepdims=True)
                m_sc[g, rs, :] = m_new

    @pl.when(mt[p] == 1)
    def _():
        step(True)

    @pl.when(mt[p] == 0)
    def _():
        step(False)

    @pl.when(lt[p] == 1)
    def _():
        for g in range(heads_per_step):
            if ones_col:
                out = acc_sc[g, :, 0:dv] / acc_sc[g, :, dv:dv + LANES]
            else:
                out = acc_sc[g] / lanes(l_sc[g], dv)
            o_ref[:, g * dv:(g + 1) * dv] = out.astype(o_ref.dtype)


def _pair_tables(nq, mode, bq, bk):
    qs, ks, ms = [], [], []
    for qi in range(nq):
        if mode == "band":
            lo = max(0, (qi * bq - LEFT_CHUNKS * CHUNK) // bk)
        else:
            lo = 0
        hi = ((qi + 1) * bq - 1) // bk
        for ki in range(lo, hi + 1):
            qs.append(qi)
            ks.append(ki)
            q_lo, q_hi = qi * bq, (qi + 1) * bq - 1
            k_lo, k_hi = ki * bk, (ki + 1) * bk - 1
            if mode == "causal":
                full = k_hi <= q_lo
            elif mode == "chunk":
                full = k_hi // CHUNK <= q_lo // CHUNK
            else:
                full = (k_hi // CHUNK <= q_lo // CHUNK
                        and k_lo // CHUNK + LEFT_CHUNKS >= q_hi // CHUNK)
            ms.append(0 if full else 1)
    n = len(qs)
    first = [1 if (i == 0 or qs[i - 1] != qs[i]) else 0 for i in range(n)]
    last = [1 if (i == n - 1 or qs[i + 1] != qs[i]) else 0 for i in range(n)]
    as_i32 = lambda a: jnp.asarray(np.array(a, dtype=np.int32))
    return as_i32(qs), as_i32(ks), as_i32(first), as_i32(last), as_i32(ms), n


def _attention(q, k, v, *, batch, seq, heads, q_off, k_off, v_off, dv, v_group, mode,
               bq, bk, out_dtype, heads_per_step=2, rsub=256,
               qx=None, kx=None, qx_off=0, bias=None):
    bq = min(bq, seq)
    bk = min(bk, seq)
    rsub = min(rsub, bq)
    nq = seq // bq
    nk = seq // bk
    G = heads_per_step
    assert heads % G == 0 and G % v_group == 0 and q_off % G == 0 and k_off % G == 0
    assert (v_off * v_group) % G == 0 and qx_off % G == 0
    vw = dv * G // v_group
    qt, kt, ft, lt, mt, npairs = _pair_tables(nq, mode, bq, bk)
    has_x = qx is not None
    x_shared = has_x and kx.shape[1] == HEAD_DIM
    has_bias = bias is not None
    ones_col = dv == LANES
    W = G * HEAD_DIM

    def im(off_blocks, row_tab, nblk):
        def f(b, h, p, qt, kt, ft, lt, mt):
            t = qt if row_tab == "q" else kt
            return (b * nblk + t[p], off_blocks + h)
        return f

    in_specs = [
        pl.BlockSpec((bq, W), im(q_off // G, "q", nq)),
        pl.BlockSpec((bk, W), im(k_off // G, "k", nk)),
        pl.BlockSpec((bk, vw), im(v_off * v_group // G, "k", nk)),
    ]
    args = [q, k, v]
    if has_x:
        in_specs.append(pl.BlockSpec((bq, W), im(qx_off // G, "q", nq)))
        if x_shared:
            in_specs.append(pl.BlockSpec(
                (bk, HEAD_DIM), lambda b, h, p, qt, kt, ft, lt, mt: (b * nk + kt[p], 0)))
        else:
            in_specs.append(pl.BlockSpec((bk, W), im(0, "k", nk)))
        args += [qx, kx]
    if has_bias:
        in_specs.append(
            pl.BlockSpec((G, 1, bq, bk), lambda b, h, p, qt, kt, ft, lt, mt: (h, qt[p] - kt[p], 0, 0)))
        args.append(bias)

    kern = functools.partial(_attn_kernel, mode=mode, bq=bq, bk=bk, heads_per_step=G, dv=dv,
                             v_group=v_group, rsub=rsub, has_x=has_x, x_shared=x_shared,
                             has_bias=has_bias, ones_col=ones_col)
    return pl.pallas_call(
        kern,
        out_shape=jax.ShapeDtypeStruct((batch * seq, heads * dv), out_dtype),
        grid_spec=pltpu.PrefetchScalarGridSpec(
            num_scalar_prefetch=5,
            grid=(batch, heads // G, npairs),
            in_specs=in_specs,
            out_specs=pl.BlockSpec((bq, G * dv),
                                   lambda b, h, p, qt, kt, ft, lt, mt: (b * nq + qt[p], h)),
            scratch_shapes=[pltpu.VMEM((G, bq, LANES), F32),
                            pltpu.VMEM((G, bq, dv + LANES if ones_col else dv), F32)]
            + ([] if ones_col else [pltpu.VMEM((G, bq, LANES), F32)]),
        ),
        compiler_params=_cparams(("parallel", "parallel", "arbitrary")),
        name="attn_" + mode,
    )(qt, kt, ft, lt, mt, *args)


def _diff_combine_kernel(a_ref, lam_ref, g_ref, o_ref, *, lam_init):
    lf = lam_ref[...]
    lam = (jnp.exp(jnp.sum(lf[0:1] * lf[1:2], axis=-1, keepdims=True))
           - jnp.exp(jnp.sum(lf[2:3] * lf[3:4], axis=-1, keepdims=True)) + lam_init)
    w = 2 * DIFF_DIM
    for hd in range(H_DIFF):
        a1 = a_ref[:, (2 * hd) * w:(2 * hd + 1) * w]
        a2 = a_ref[:, (2 * hd + 1) * w:(2 * hd + 2) * w]
        d = a1 - lam * a2
        d = d * lax.rsqrt(jnp.mean(d * d, axis=-1, keepdims=True) + RMS_EPS) * g_ref[...]
        o_ref[:, hd * w:(hd + 1) * w] = (d * (1.0 - lam_init)).astype(o_ref.dtype)


def _diff_combine(a, diff_lambda, g_subln, lam_init, bm=512):
    M = a.shape[0]
    bm = min(bm, M)
    w = 2 * DIFF_DIM
    return pl.pallas_call(
        functools.partial(_diff_combine_kernel, lam_init=lam_init),
        out_shape=jax.ShapeDtypeStruct((M, H_DIFF * w), BF16),
        grid=(M // bm,),
        in_specs=[pl.BlockSpec((bm, 2 * H_DIFF * w), lambda i: (i, 0)),
                  pl.BlockSpec((4, DIFF_DIM), lambda i: (0, 0)),
                  pl.BlockSpec((1, w), lambda i: (0, 0))],
        out_specs=pl.BlockSpec((bm, H_DIFF * w), lambda i: (i, 0)),
        compiler_params=_cparams(("parallel",)),
        name="diff_combine",
    )(a, diff_lambda.astype(F32), g_subln.reshape(1, w).astype(F32))


PEER_SUB = 256


def _top_values(s, n):
    vals = []
    work = s
    for r in range(n):
        m = jnp.max(work, axis=0, keepdims=True)
        vals.append(m)
        if r + 1 < n:
            work = jnp.where(work >= m, -jnp.inf, work)
    return vals


def _peer_stats_kernel(q_ref, keys_ref, s2_ref, c_ref, tau_ref):
    nt = (((1,), (1,)), ((), ()))
    tb = q_ref.shape[0]
    k1 = keys_ref[0, 0]
    k2 = keys_ref[0, 1]
    for t0 in range(0, tb, PEER_SUB):
        q1 = q_ref[t0:t0 + PEER_SUB, 0:PEER_HALF]
        q2 = q_ref[t0:t0 + PEER_SUB, PEER_HALF:2 * PEER_HALF]
        s1 = lax.dot_general(k1, q1, nt, preferred_element_type=F32)
        s2 = lax.dot_general(k2, q2, nt, preferred_element_type=F32)
        sv1 = _top_values(s1, PEER_TOPK)
        sv2 = jnp.concatenate(_top_values(s2, PEER_TOPK), axis=0)
        rank = lax.broadcasted_iota(jnp.int32, (8, PEER_SUB), 0)
        sv2_lo = sv2[0:8]
        parts = [sv1[0] + sv2, sv1[1] + sv2_lo]
        for a in range(2, PEER_TOPK):
            parts.append(jnp.where(rank < PEER_TOPK // (a + 1), sv1[a] + sv2_lo, -jnp.inf))
        cand = jnp.concatenate(parts, axis=0)
        top = _top_values(cand, PEER_TOPK + 1)
        cmax = top[0]
        tau = 0.5 * (top[PEER_TOPK - 1] + top[PEER_TOPK])
        z = jnp.sum(jnp.where(cand >= tau, jnp.exp(cand - cmax), 0.0), axis=0, keepdims=True)
        lse = cmax + jnp.log(z)
        s2_ref[0, :, t0:t0 + PEER_SUB] = s2 * LOG2E
        c_ref[0, :, t0:t0 + PEER_SUB] = (s1 - lse) * LOG2E
        tau_ref[0, :, t0:t0 + PEER_SUB] = (tau - lse) * LOG2E


def _peer_stats(q, keys, tb=1024):
    T = q.shape[0]
    tb = min(tb, T)
    assert T % tb == 0 and tb % PEER_SUB == 0
    big = jax.ShapeDtypeStruct((PEER_HEADS, N_KEYS, T), F32)
    return pl.pallas_call(
        _peer_stats_kernel,
        out_shape=(big, big, jax.ShapeDtypeStruct((PEER_HEADS, 1, T), F32)),
        grid=(T // tb, PEER_HEADS),
        in_specs=[pl.BlockSpec((tb, 2 * PEER_HALF), lambda i, h: (i, h)),
                  pl.BlockSpec((1, 2, N_KEYS, PEER_HALF), lambda i, h: (h, 0, 0, 0))],
        out_specs=(pl.BlockSpec((1, N_KEYS, tb), lambda i, h: (h, 0, i)),
                   pl.BlockSpec((1, N_KEYS, tb), lambda i, h: (h, 0, i)),
                   pl.BlockSpec((1, 1, tb), lambda i, h: (h, 0, i))),
        compiler_params=_cparams(("parallel", "parallel")),
        name="peer_stats",
    )(q, keys)


GELU_K = math.sqrt(2.0 / math.pi)
GATE_ROWS = 32
MIX_TOK = 128
MIX_A_UNITS = 2
MIX_C_UNITS = 16


def _peer_mix_body(xT_ref, u_ref, vT_ref, s2_ref, c_ref, tau_ref, o_ref,
                   h_wr, h_rd, w_wr, w_rd, ni):
    tb = xT_ref.shape[1]
    d = vT_ref.shape[0]

    def gate_sub(il, t0, k0):
        ts = slice(t0, t0 + MIX_TOK)
        ks = slice(k0, k0 + GATE_ROWS)
        rows = slice(il * N_KEYS + k0, il * N_KEYS + k0 + GATE_ROWS)
        g = None
        for h in range(PEER_HEADS):
            z = s2_ref[h, ks, ts] + c_ref[h, 0, il:il + 1, ts]
            e = jnp.where(z >= tau_ref[h, :, ts], jnp.exp2(z), 0.0)
            g = e if g is None else g + e
        x = h_rd[rows, ts]
        inner = x * ((x * x) * (0.044715 * GELU_K) + GELU_K)
        hx = 0.5 * x
        act = hx + hx * jnp.tanh(inner)
        w_wr[rows, ts] = (g * act).astype(BF16)

    ec = u_ref.shape[0]
    a_part = ec // MIX_A_UNITS
    c_part = d // MIX_C_UNITS
    for t0 in range(0, tb, 2 * MIX_TOK):
        tq = slice(t0, t0 + 2 * MIX_TOK)

        def unit_a(j):
            h_wr[j * a_part:(j + 1) * a_part, tq] = jnp.dot(
                u_ref[j * a_part:(j + 1) * a_part, :], xT_ref[:, tq], preferred_element_type=F32)

        def unit_c(j):
            o_ref[j * c_part:(j + 1) * c_part, tq] += jnp.dot(
                vT_ref[j * c_part:(j + 1) * c_part, :], w_rd[:, tq], preferred_element_type=F32)

        mxu = []
        c_per_a = MIX_C_UNITS // MIX_A_UNITS
        for j in range(MIX_A_UNITS):
            mxu.append(functools.partial(unit_a, j))
            mxu += [functools.partial(unit_c, j * c_per_a + i) for i in range(c_per_a)]
        vpu = [functools.partial(gate_sub, r, t0 + dt, k0)
               for r in range(ni) for dt in (0, MIX_TOK) for k0 in range(0, N_KEYS, GATE_ROWS)]
        done = 0
        for i, unit in enumerate(mxu):
            unit()
            upto = ((i + 1) * len(vpu)) // len(mxu)
            for v in vpu[done:upto]:
                v()
            done = upto


def _peer_mix_kernel(xT_ref, u_ref, vT_ref, s2_ref, c_ref, tau_ref, o_ref, h0, h1, w0, w1, *, ni):
    c = pl.program_id(1)

    @pl.when(c == 0)
    def _():
        o_ref[...] = jnp.zeros(o_ref.shape, F32)
        h1[...] = jnp.zeros(h1.shape, F32)
        w0[...] = jnp.zeros(w0.shape, BF16)

    args = (xT_ref, u_ref, vT_ref, s2_ref, c_ref, tau_ref, o_ref)

    @pl.when(lax.rem(c, 2) == 0)
    def _():
        _peer_mix_body(*args, h0, h1, w1, w0, ni)

    @pl.when(lax.rem(c, 2) == 1)
    def _():
        _peer_mix_body(*args, h1, h0, w0, w1, ni)


def _peer_mix(xT, u, vT, s2, c, tau, *, tb=512, ec=512):
    D, T = xT.shape
    E = u.shape[0]
    tb = min(tb, T)
    assert T % tb == 0 and E % ec == 0 and ec % N_KEYS == 0 and tb % (2 * MIX_TOK) == 0
    ni = ec // N_KEYS
    nc = E // ec
    once = pl.Buffered(1)
    return pl.pallas_call(
        functools.partial(_peer_mix_kernel, ni=ni),
        out_shape=jax.ShapeDtypeStruct((D, T), F32),
        grid=(T // tb, nc + 2),
        in_specs=[pl.BlockSpec((D, tb), lambda i, e: (0, i), pipeline_mode=once),
                  pl.BlockSpec((ec, D), lambda i, e: (jnp.minimum(e, nc - 1), 0)),
                  pl.BlockSpec((D, ec), lambda i, e: (0, jnp.clip(e - 2, 0, nc - 1))),
                  pl.BlockSpec((PEER_HEADS, N_KEYS, tb), lambda i, e: (0, 0, i), pipeline_mode=once),
                  pl.BlockSpec((PEER_HEADS, 1, ni, tb), lambda i, e: (0, jnp.clip(e - 1, 0, nc - 1), 0, i)),
                  pl.BlockSpec((PEER_HEADS, 1, tb), lambda i, e: (0, 0, i), pipeline_mode=once)],
        out_specs=pl.BlockSpec((D, tb), lambda i, e: (0, i)),
        scratch_shapes=[pltpu.VMEM((ec, tb), F32), pltpu.VMEM((ec, tb), F32),
                        pltpu.VMEM((ec, tb), BF16), pltpu.VMEM((ec, tb), BF16)],
        compiler_params=_cparams(("parallel", "arbitrary")),
        name="peer_mix",
    )(xT, u, vT, s2, c.reshape(PEER_HEADS, nc, ni, T), tau)


def _rope_tables(seq, d):
    inv = ROPE_THETA ** (-jnp.arange(0, d, 2, dtype=F32) / d)
    ang = jnp.arange(seq, dtype=F32)[:, None] * inv[None, :]
    cos, sin = jnp.cos(ang), jnp.sin(ang)
    half = d // 2
    pad = jnp.zeros((seq, LANES - d), F32)
    t0 = jnp.concatenate([cos, cos, pad], axis=1)
    if d == LANES:
        return (half,), (t0, jnp.concatenate([-sin, sin], axis=1))
    zh = jnp.zeros((seq, half), F32)
    t1 = jnp.concatenate([-sin, zh, pad], axis=1)
    t2 = jnp.concatenate([zh, sin, pad], axis=1)
    return (LANES - half, half), (t0, t1, t2)


def _band_bias_tiles(rel_bias, bq, bk):
    assert bq == bk
    n = bq
    nd = (LEFT_CHUNKS * CHUNK + bq - 1) // bk + 1
    m = np.arange(2 * n)
    tiles = []
    for dlt in range(nd):
        rel = np.where(m < n, dlt * n - m, dlt * n + 2 * n - m)
        ridx = np.clip(rel, -REL_MAX, REL_MAX) + REL_MAX
        pat = jnp.take(rel_bias.astype(F32), jnp.asarray(ridx), axis=1)
        nh = pat.shape[0]
        flat = jnp.broadcast_to(pat[:, None, :], (nh, n, 2 * n)).reshape(nh, n * 2 * n)
        tiles.append(flat[:, :n * (2 * n - 1)].reshape(nh, n, 2 * n - 1)[:, :, :n])
    return jnp.stack(tiles, axis=1)


def _decay_columns(cum):
    def top_bits(a):
        bits = lax.bitcast_convert_type(a, jnp.uint32) & jnp.uint32(0xFFFF0000)
        return lax.bitcast_convert_type(bits, F32)

    hi_f = top_bits(cum)
    r1 = cum - hi_f
    mid_f = top_bits(r1)
    hi, mid, lo = hi_f.astype(BF16), mid_f.astype(BF16), (r1 - mid_f).astype(BF16)
    one = jnp.ones_like(hi)
    pad = jnp.zeros(cum.shape + (HEAD_DIM - 6,), BF16)
    qx = jnp.concatenate([jnp.stack([hi, mid, lo, one, one, one], axis=-1), pad], axis=-1)
    kx = jnp.concatenate([jnp.stack([one, one, one, -hi, -mid, -lo], axis=-1), pad], axis=-1)
    t = cum.shape[0]
    return qx.reshape(t, -1), kx.reshape(t, -1)


def _even_layer(xb, w_in, b_forget, rel_bias, w_out, batch, seq):
    hd = HEAD_DIM
    scale = hd ** -0.5 * LOG2E
    n = H_FOX * hd
    fq, fk, fv = w_in[:, 0:n], w_in[:, n:2 * n], w_in[:, 2 * n:3 * n]
    wf = w_in[:, 3 * n:3 * n + H_FOX]
    o = 3 * n + H_FOX
    m = H_CHK * hd
    cq, ck, cv = w_in[:, o:o + m], w_in[:, o + m:o + 2 * m], w_in[:, o + 2 * m:o + 3 * m]
    w_main = jnp.concatenate([fq, cq, fk, fv, ck, cv], axis=1).astype(BF16)
    col_scale = jnp.concatenate([jnp.full((n + m,), scale, F32), jnp.ones((2 * n + 2 * m,), F32)])
    proj = _mm(xb, w_main, out_dtype=BF16, col_scale=col_scale)

    wf_pad = jnp.pad(wf, ((0, 0), (0, LANES - H_FOX))).astype(BF16)
    logits = _mm(xb, wf_pad, out_dtype=F32)
    bias_row = jnp.pad(b_forget.astype(F32), (0, LANES - H_FOX)).reshape(1, LANES)
    cum = _gate_scan(logits, bias_row, batch, seq)
    qx, kx = _decay_columns(cum[:, :H_FOX] * LOG2E)

    nh = n // hd
    o_fox = _attention(proj, proj, proj, batch=batch, seq=seq, heads=H_FOX,
                       q_off=0, k_off=2 * nh, v_off=3 * nh, dv=hd, v_group=1, mode="causal",
                       bq=512, bk=512, out_dtype=BF16, qx=qx, kx=kx)
    bqb = min(256, seq)
    bias = _band_bias_tiles(rel_bias * LOG2E, bqb, bqb)
    o_chk = _attention(proj, proj, proj, batch=batch, seq=seq, heads=H_CHK,
                       q_off=nh, k_off=4 * nh, v_off=5 * nh, dv=hd, v_group=1, mode="band",
                       bq=bqb, bk=bqb, out_dtype=BF16, bias=bias)
    o_cat = jnp.concatenate([o_fox, o_chk], axis=1)
    return _mm(o_cat, w_out.astype(BF16), out_dtype=F32)


def _odd_layer(xb, w_in, g_q_lora, g_kv_lora, w_uq, w_ukv, diff_lambda, g_subln, w_out,
               layer_idx, batch, seq):
    o1 = Q_LORA + KV_LORA
    o2 = o1 + MLA_ROPE
    nd = H_DIFF * 2 * DIFF_DIM
    w_lora = w_in[:, :o1].astype(BF16)
    w_kpe = jnp.pad(w_in[:, o1:o2], ((0, 0), (0, LANES - MLA_ROPE))).astype(BF16)
    w_dqk = w_in[:, o2:o2 + 2 * nd].astype(BF16)
    w_dv = w_in[:, o2 + 2 * nd:o2 + 3 * nd].astype(BF16)

    rope128 = _rope_tables(seq, DIFF_DIM)
    rope64 = _rope_tables(seq, MLA_ROPE)

    lora = _mm(xb, w_lora, out_dtype=F32)
    kpe = _mm(xb, w_kpe, out_dtype=BF16, rope=rope64)
    dscale = jnp.concatenate([jnp.full((nd,), DIFF_DIM ** -0.5 * LOG2E, F32), jnp.ones((nd,), F32)])
    dqk = _mm(xb, w_dqk, out_dtype=BF16, col_scale=dscale, rope=rope128)
    dv = _mm(xb, w_dv, out_dtype=BF16)

    mscale = (MLA_NOPE + MLA_ROPE) ** -0.5 * LOG2E
    wq = w_uq.reshape(Q_LORA, H_MLA, MLA_NOPE + MLA_ROPE)
    wq_nope = wq[:, :, :MLA_NOPE].reshape(Q_LORA, H_MLA * MLA_NOPE).astype(BF16)
    wq_rope = jnp.pad(wq[:, :, MLA_NOPE:], ((0, 0), (0, 0), (0, LANES - MLA_ROPE)))
    wq_rope = wq_rope.reshape(Q_LORA, H_MLA * LANES).astype(BF16)
    cq = lora[:, :Q_LORA]
    ckv = lora[:, Q_LORA:]
    ms = jnp.full((H_MLA * LANES,), mscale, F32)
    q_nope = _mm(cq, wq_nope, out_dtype=BF16, gain=g_q_lora, col_scale=ms)
    q_rope = _mm(cq, wq_rope, out_dtype=BF16, gain=g_q_lora, col_scale=ms, rope=rope64)
    wkv = w_ukv.reshape(KV_LORA, H_MLA, MLA_NOPE + MLA_V)
    wkv = jnp.concatenate([wkv[:, :, :MLA_NOPE].reshape(KV_LORA, H_MLA * MLA_NOPE),
                           wkv[:, :, MLA_NOPE:].reshape(KV_LORA, H_MLA * MLA_V)], axis=1)
    kv = _mm(ckv, wkv.astype(BF16), out_dtype=BF16, gain=g_kv_lora)

    o_mla = _attention(q_nope, kv, kv, batch=batch, seq=seq, heads=H_MLA,
                       q_off=0, k_off=0, v_off=H_MLA, dv=MLA_V, v_group=1, mode="chunk",
                       bq=512, bk=512, out_dtype=BF16, qx=q_rope, kx=kpe)
    a = _attention(dqk, dqk, dv, batch=batch, seq=seq, heads=2 * H_DIFF,
                   q_off=0, k_off=2 * H_DIFF, v_off=0, dv=2 * DIFF_DIM, v_group=2, mode="chunk",
                   bq=512, bk=512, out_dtype=F32)
    lam_init = 0.8 - 0.6 * math.exp(-0.3 * layer_idx)
    o_diff = _diff_combine(a, diff_lambda, g_subln, lam_init)
    o_cat = jnp.concatenate([o_mla, o_diff], axis=1)
    return _mm(o_cat, w_out.astype(BF16), out_dtype=F32)


def _peer_layer(xb, xbT, w_query, sub_keys, u_tab, v_tab):
    q = _mm(xb, w_query.astype(BF16), out_dtype=BF16)
    s2, c, tau = _peer_stats(q, sub_keys.astype(BF16))
    return _peer_mix(xbT, u_tab.astype(BF16), v_tab.T.astype(BF16), s2, c, tau)


def kernel(x, w_in_even, b_forget, rel_bias, w_out_even, w_in_odd, g_q_lora, g_kv_lora, w_uq, w_ukv, diff_lambda, g_subln, w_out_odd, peer_w_query, peer_sub_keys, peer_u, peer_v, ln_mix_g, ln_mix_b, ln_ffn_g, ln_ffn_b):
    batch, seq, d = x.shape
    depth = ln_mix_g.shape[0]
    alpha = (2 * depth) ** 0.25
    xf = x.reshape(batch * seq, d).astype(F32)
    xb = xf.astype(BF16)
    for l in range(depth):
        i = l // 2
        if l % 2 == 0:
            y = _even_layer(xb, w_in_even[i], b_forget[i], rel_bias[i], w_out_even[i], batch, seq)
        else:
            y = _odd_layer(xb, w_in_odd[i], g_q_lora[i], g_kv_lora[i], w_uq[i], w_ukv[i],
                           diff_lambda[i], g_subln[i], w_out_odd[i], l, batch, seq)
        xf, xb, xbT = _res_ln(xf, y, ln_mix_g[l], ln_mix_b[l], alpha, emit_transposed=True)
        yT = _peer_layer(xb, xbT, peer_w_query[l], peer_sub_keys[l], peer_u[l], peer_v[l])
        xf, xb = _res_ln(xf, yT, ln_ffn_g[l], ln_ffn_b[l], alpha, y_transposed=True)
    return xf.reshape(batch, seq, d)
```

```python
import functools
import math

import numpy as np
import jax
import jax.numpy as jnp
from jax import lax
from jax.experimental import pallas as pl
from jax.experimental.pallas import tpu as pltpu

F32 = jnp.float32
BF16 = jnp.bfloat16

CHUNK = 64
HEAD_DIM = 128
ROPE_THETA = 10000.0
LN_EPS = 1e-5
RMS_EPS = 1e-6
NEG_INF = -1e30
H_FOX = 8
H_CHK = 8
LEFT_CHUNKS = 8
REL_MAX = 128
H_MLA = 8
Q_LORA = 512
KV_LORA = 512
MLA_NOPE = 128
MLA_ROPE = 64
MLA_V = 128
H_DIFF = 4
DIFF_DIM = 128
N_KEYS = 128
PEER_HEADS = 8
PEER_TOPK = 16
PEER_HALF = 128

LOG2E = math.log2(math.e)
LANES = 128
VMEM_LIMIT = 56 * 1024 * 1024


def _cparams(sem):
    return pltpu.CompilerParams(dimension_semantics=sem, vmem_limit_bytes=VMEM_LIMIT)


def _mm_kernel(*refs, has_gain, has_scale, rope_shifts, n_tabs):
    it = iter(refs)
    x_ref = next(it)
    w_ref = next(it)
    g_ref = next(it) if has_gain else None
    sc_ref = next(it) if has_scale else None
    tabs = [next(it) for _ in range(n_tabs)]
    o_ref = next(it)

    x = x_ref[...]
    if has_gain:
        xf = x.astype(F32)
        xf = xf * lax.rsqrt(jnp.mean(xf * xf, axis=-1, keepdims=True) + RMS_EPS) * g_ref[...]
        x = xf
    x = x.astype(BF16)
    acc = jnp.dot(x, w_ref[...], preferred_element_type=F32)
    if has_scale:
        acc = acc * sc_ref[...]
    if n_tabs:
        t = [r[...] for r in tabs]
        for j in range(acc.shape[1] // LANES):
            y = acc[:, j * LANES:(j + 1) * LANES]
            out = y * t[0]
            for k, s in enumerate(rope_shifts):
                out = out + pltpu.roll(y, s, 1) * t[k + 1]
            o_ref[:, j * LANES:(j + 1) * LANES] = out.astype(o_ref.dtype)
    else:
        o_ref[...] = acc.astype(o_ref.dtype)


def _mm(x, w, *, out_dtype, gain=None, col_scale=None, rope=None, bm=1024, bn=1024):
    M, K = x.shape
    N = w.shape[1]
    bm = min(bm, M)
    bn = min(bn, N)
    if rope is not None:
        bm = min(bm, rope[1][0].shape[0])
    assert M % bm == 0 and N % bn == 0
    in_specs = [pl.BlockSpec((bm, K), lambda i, j: (i, 0)),
                pl.BlockSpec((K, bn), lambda i, j: (0, j))]
    args = [x, w]
    if gain is not None:
        in_specs.append(pl.BlockSpec((1, K), lambda i, j: (0, 0)))
        args.append(gain.reshape(1, K).astype(F32))
    if col_scale is not None:
        in_specs.append(pl.BlockSpec((1, bn), lambda i, j: (0, j)))
        args.append(col_scale.reshape(1, N).astype(F32))
    shifts, tabs = ((), ())
    if rope is not None:
        shifts, tabs = rope
        seq = tabs[0].shape[0]
        assert seq % bm == 0
        nsb = seq // bm
        for t in tabs:
            in_specs.append(pl.BlockSpec((bm, LANES), lambda i, j: (i % nsb, 0)))
            args.append(t)
    kern = functools.partial(_mm_kernel, has_gain=gain is not None,
                             has_scale=col_scale is not None,
                             rope_shifts=tuple(shifts), n_tabs=len(tabs))
    return pl.pallas_call(
        kern,
        out_shape=jax.ShapeDtypeStruct((M, N), out_dtype),
        grid=(M // bm, N // bn),
        in_specs=in_specs,
        out_specs=pl.BlockSpec((bm, bn), lambda i, j: (i, j)),
        compiler_params=_cparams(("parallel", "parallel")),
        name="mm",
    )(*args)


def _res_ln_kernel(x_ref, y_ref, g_ref, b_ref, o_ref, ob_ref, obt_ref, *, alpha, y_transposed):
    y = y_ref[...]
    if y_transposed:
        y = y.T
    z = alpha * x_ref[...] + y
    mu = jnp.mean(z, axis=-1, keepdims=True)
    zc = z - mu
    var = jnp.mean(zc * zc, axis=-1, keepdims=True)
    out = zc * lax.rsqrt(var + LN_EPS) * g_ref[...] + b_ref[...]
    o_ref[...] = out
    ob_ref[...] = out.astype(BF16)
    if obt_ref is not None:
        obt_ref[...] = out.T.astype(BF16)


def _res_ln(x, y, g, b, alpha, *, y_transposed=False, emit_transposed=False, bm=512):
    M, D = x.shape
    bm = min(bm, M)
    row = pl.BlockSpec((bm, D), lambda i: (i, 0))
    col = pl.BlockSpec((D, bm), lambda i: (0, i))
    vec = pl.BlockSpec((1, D), lambda i: (0, 0))
    out_shape = [jax.ShapeDtypeStruct((M, D), F32), jax.ShapeDtypeStruct((M, D), BF16)]
    out_specs = [row, row]
    if emit_transposed:
        out_shape.append(jax.ShapeDtypeStruct((D, M), BF16))
        out_specs.append(col)
        kern = functools.partial(_res_ln_kernel, alpha=alpha, y_transposed=y_transposed)
    else:
        kern = functools.partial(_res_ln_kernel, obt_ref=None, alpha=alpha, y_transposed=y_transposed)
    return pl.pallas_call(
        kern,
        out_shape=tuple(out_shape),
        grid=(M // bm,),
        in_specs=[row, col if y_transposed else row, vec, vec],
        out_specs=tuple(out_specs),
        compiler_params=_cparams(("parallel",)),
        name="res_ln",
    )(x, y, g.reshape(1, D), b.reshape(1, D))


def _proj_res_ln_kernel(a_ref, b_ref, w_ref, x_ref, g_ref, be_ref, o_ref, ob_ref, obt_ref, *, alpha):
    ka = a_ref.shape[1]
    y = (jnp.dot(a_ref[...], w_ref[0:ka, :], preferred_element_type=F32)
         + jnp.dot(b_ref[...], w_ref[ka:, :], preferred_element_type=F32))
    z = alpha * x_ref[...] + y
    mu = jnp.mean(z, axis=-1, keepdims=True)
    zc = z - mu
    var = jnp.mean(zc * zc, axis=-1, keepdims=True)
    out = zc * lax.rsqrt(var + LN_EPS) * g_ref[...] + be_ref[...]
    o_ref[...] = out
    ob_ref[...] = out.astype(BF16)
    obt_ref[...] = out.T.astype(BF16)


def _proj_res_ln(a, b2, w, x, g, b, alpha, bm=512):
    M, D = x.shape
    ka, kb = a.shape[1], b2.shape[1]
    bm = min(bm, M)
    row = pl.BlockSpec((bm, D), lambda i: (i, 0))
    vec = pl.BlockSpec((1, D), lambda i: (0, 0))
    return pl.pallas_call(
        functools.partial(_proj_res_ln_kernel, alpha=alpha),
        out_shape=(jax.ShapeDtypeStruct((M, D), F32), jax.ShapeDtypeStruct((M, D), BF16),
                   jax.ShapeDtypeStruct((D, M), BF16)),
        grid=(M // bm,),
        in_specs=[pl.BlockSpec((bm, ka), lambda i: (i, 0)),
                  pl.BlockSpec((bm, kb), lambda i: (i, 0)),
                  pl.BlockSpec((ka + kb, D), lambda i: (0, 0), pipeline_mode=pl.Buffered(1)),
                  row, vec, vec],
        out_specs=(row, row, pl.BlockSpec((D, bm), lambda i: (0, i))),
        compiler_params=_cparams(("parallel",)),
        name="proj_res_ln",
    )(a, b2, w, x, g.reshape(1, D), b.reshape(1, D))


SCAN_BLOCK = 256


def _gate_scan_kernel(z_ref, b_ref, o_ref):
    seq = z_ref.shape[0]
    r = lax.broadcasted_iota(jnp.int32, (SCAN_BLOCK, SCAN_BLOCK), 0)
    c = lax.broadcasted_iota(jnp.int32, (SCAN_BLOCK, SCAN_BLOCK), 1)
    tri = jnp.where(c <= r, 1.0, 0.0).astype(F32)
    carry = jnp.zeros((1, LANES), F32)
    for i in range(seq // SCAN_BLOCK):
        z = z_ref[i * SCAN_BLOCK:(i + 1) * SCAN_BLOCK, :] + b_ref[...]
        logf = jnp.minimum(z, 0.0) - jnp.log(1.0 + jnp.exp(-jnp.abs(z)))
        cs = jnp.dot(tri, logf, preferred_element_type=F32,
                     precision=lax.Precision.HIGHEST) + carry
        o_ref[i * SCAN_BLOCK:(i + 1) * SCAN_BLOCK, :] = cs
        carry = cs[SCAN_BLOCK - 1:SCAN_BLOCK, :]


def _gate_scan(z, bias_row, batch, seq):
    assert seq % SCAN_BLOCK == 0
    return pl.pallas_call(
        _gate_scan_kernel,
        out_shape=jax.ShapeDtypeStruct((batch * seq, LANES), F32),
        grid=(batch,),
        in_specs=[pl.BlockSpec((seq, LANES), lambda b: (b, 0)),
                  pl.BlockSpec((1, LANES), lambda b: (0, 0))],
        out_specs=pl.BlockSpec((seq, LANES), lambda b: (b, 0)),
        compiler_params=_cparams(("parallel",)),
        name="gate_scan",
    )(z, bias_row)


def _attn_kernel(*refs, mode, bq, bk, heads_per_step, dv, v_group, rsub, has_x, x_shared,
                 has_bias, ones_col):
    qt, kt, ft, lt, mt = refs[:5]
    it = iter(refs[5:])
    q_ref = next(it)
    k_ref = next(it)
    v_ref = next(it)
    qx_ref = next(it) if has_x else None
    kx_ref = next(it) if has_x else None
    bias_ref = next(it) if has_bias else None
    o_ref = next(it)
    m_sc = next(it)
    acc_sc = next(it)
    l_sc = None if ones_col else next(it)

    p = pl.program_id(2)
    qi = qt[p]
    ki = kt[p]
    accw = acc_sc.shape[2]

    @pl.when(ft[p] == 1)
    def _():
        m_sc[...] = jnp.full(m_sc.shape, NEG_INF, F32)
        acc_sc[...] = jnp.zeros(acc_sc.shape, F32)
        if l_sc is not None:
            l_sc[...] = jnp.zeros(l_sc.shape, F32)

    nt = (((1,), (1,)), ((), ()))

    def lanes(x, width):
        return x if width == LANES else jnp.concatenate([x] * (width // LANES), axis=1)

    def step(masked):
        for g in range(heads_per_step):
            hs = slice(g * HEAD_DIM, (g + 1) * HEAD_DIM)
            vs = slice((g // v_group) * dv, (g // v_group + 1) * dv)
            kk = k_ref[:, hs]
            if has_x:
                kk = jnp.concatenate([kk, kx_ref[...] if x_shared else kx_ref[:, hs]], axis=1)
            vv = v_ref[:, vs]
            if ones_col:
                vv = jnp.concatenate([vv, jnp.ones((bk, LANES), BF16)], axis=1)
            for r0 in range(0, bq, rsub):
                rs = slice(r0, r0 + rsub)
                qq = q_ref[rs, hs]
                if has_x:
                    qq = jnp.concatenate([qq, qx_ref[rs, hs]], axis=1)
                s = lax.dot_general(qq, kk, nt, preferred_element_type=F32)
                if has_bias:
                    s = s + bias_ref[g, 0, rs, :]
                if masked:
                    rows = qi * bq + r0 + lax.broadcasted_iota(jnp.int32, (rsub, bk), 0)
                    cols = ki * bk + lax.broadcasted_iota(jnp.int32, (rsub, bk), 1)
                    if mode == "causal":
                        allowed = cols <= rows
                    else:
                        qc = lax.shift_right_logical(rows, 6)
                        kc = lax.shift_right_logical(cols, 6)
                        if mode == "chunk":
                            allowed = kc <= qc
                        else:
                            dist = qc - kc
                            allowed = dist * (LEFT_CHUNKS - dist) >= 0
                    s = jnp.where(allowed, s, NEG_INF)
                m_prev = m_sc[g, rs, :]
                m_new = jnp.maximum(m_prev, jnp.max(s, axis=1, keepdims=True))
                alpha = jnp.exp2(m_prev - m_new)
                pr = jnp.exp2(s - lanes(m_new, bk))
                pv = jnp.dot(pr.astype(BF16), vv, preferred_element_type=F32)
                acc_sc[g, rs, :] = lanes(alpha, accw) * acc_sc[g, rs, :] + pv
                if l_sc is not None:
                    l_sc[g, rs, :] = alpha * l_sc[g, rs, :] + jnp.sum(pr, axis=1, keepdims=True)
                m_sc[g, rs, :] = m_new

    @pl.when(mt[p] == 1)
    def _():
        step(True)

    @pl.when(mt[p] == 0)
    def _():
        step(False)

    @pl.when(lt[p] == 1)
    def _():
        for g in range(heads_per_step):
            if ones_col:
                out = acc_sc[g, :, 0:dv] / acc_sc[g, :, dv:dv + LANES]
            else:
                out = acc_sc[g] / lanes(l_sc[g], dv)
            o_ref[:, g * dv:(g + 1) * dv] = out.astype(o_ref.dtype)


def _pair_tables(nq, mode, bq, bk):
    qs, ks, ms = [], [], []
    for qi in range(nq):
        if mode == "band":
            lo = max(0, (qi * bq - LEFT_CHUNKS * CHUNK) // bk)
        else:
            lo = 0
        hi = ((qi + 1) * bq - 1) // bk
        for ki in range(lo, hi + 1):
            qs.append(qi)
            ks.append(ki)
            q_lo, q_hi = qi * bq, (qi + 1) * bq - 1
            k_lo, k_hi = ki * bk, (ki + 1) * bk - 1
            if mode == "causal":
                full = k_hi <= q_lo
            elif mode == "chunk":
                full = k_hi // CHUNK <= q_lo // CHUNK
            else:
                full = (k_hi // CHUNK <= q_lo // CHUNK
                        and k_lo // CHUNK + LEFT_CHUNKS >= q_hi // CHUNK)
            ms.append(0 if full else 1)
    n = len(qs)
    first = [1 if (i == 0 or qs[i - 1] != qs[i]) else 0 for i in range(n)]
    last = [1 if (i == n - 1 or qs[i + 1] != qs[i]) else 0 for i in range(n)]
    as_i32 = lambda a: jnp.asarray(np.array(a, dtype=np.int32))
    return as_i32(qs), as_i32(ks), as_i32(first), as_i32(last), as_i32(ms), n


def _attention(q, k, v, *, batch, seq, heads, q_off, k_off, v_off, dv, v_group, mode,
               bq, bk, out_dtype, heads_per_step=4, rsub=256,
               qx=None, kx=None, qx_off=0, bias=None):
    bq = min(bq, seq)
    bk = min(bk, seq)
    rsub = min(rsub, bq)
    nq = seq // bq
    nk = seq // bk
    G = heads_per_step
    assert heads % G == 0 and G % v_group == 0 and q_off % G == 0 and k_off % G == 0
    assert (v_off * v_group) % G == 0 and qx_off % G == 0
    vw = dv * G // v_group
    qt, kt, ft, lt, mt, npairs = _pair_tables(nq, mode, bq, bk)
    has_x = qx is not None
    x_shared = has_x and kx.shape[1] == HEAD_DIM
    has_bias = bias is not None
    ones_col = dv == LANES
    W = G * HEAD_DIM

    def im(off_blocks, row_tab, nblk):
        def f(b, h, p, qt, kt, ft, lt, mt):
            t = qt if row_tab == "q" else kt
            return (b * nblk + t[p], off_blocks + h)
        return f

    in_specs = [
        pl.BlockSpec((bq, W), im(q_off // G, "q", nq)),
        pl.BlockSpec((bk, W), im(k_off // G, "k", nk)),
        pl.BlockSpec((bk, vw), im(v_off * v_group // G, "k", nk)),
    ]
    args = [q, k, v]
    if has_x:
        in_specs.append(pl.BlockSpec((bq, W), im(qx_off // G, "q", nq)))
        if x_shared:
            in_specs.append(pl.BlockSpec(
                (bk, HEAD_DIM), lambda b, h, p, qt, kt, ft, lt, mt: (b * nk + kt[p], 0)))
        else:
            in_specs.append(pl.BlockSpec((bk, W), im(0, "k", nk)))
        args += [qx, kx]
    if has_bias:
        in_specs.append(
            pl.BlockSpec((G, 1, bq, bk), lambda b, h, p, qt, kt, ft, lt, mt: (h, qt[p] - kt[p], 0, 0)))
        args.append(bias)

    kern = functools.partial(_attn_kernel, mode=mode, bq=bq, bk=bk, heads_per_step=G, dv=dv,
                             v_group=v_group, rsub=rsub, has_x=has_x, x_shared=x_shared,
                             has_bias=has_bias, ones_col=ones_col)
    return pl.pallas_call(
        kern,
        out_shape=jax.ShapeDtypeStruct((batch * seq, heads * dv), out_dtype),
        grid_spec=pltpu.PrefetchScalarGridSpec(
            num_scalar_prefetch=5,
            grid=(batch, heads // G, npairs),
            in_specs=in_specs,
            out_specs=pl.BlockSpec((bq, G * dv),
                                   lambda b, h, p, qt, kt, ft, lt, mt: (b * nq + qt[p], h)),
            scratch_shapes=[pltpu.VMEM((G, bq, LANES), F32),
                            pltpu.VMEM((G, bq, dv + LANES if ones_col else dv), F32)]
            + ([] if ones_col else [pltpu.VMEM((G, bq, LANES), F32)]),
        ),
        compiler_params=_cparams(("parallel", "parallel", "arbitrary")),
        name="attn_" + mode,
    )(qt, kt, ft, lt, mt, *args)


def _diff_combine_kernel(a_ref, lam_ref, g_ref, o_ref, *, lam_init):
    lf = lam_ref[...]
    lam = (jnp.exp(jnp.sum(lf[0:1] * lf[1:2], axis=-1, keepdims=True))
           - jnp.exp(jnp.sum(lf[2:3] * lf[3:4], axis=-1, keepdims=True)) + lam_init)
    w = 2 * DIFF_DIM
    for hd in range(H_DIFF):
        a1 = a_ref[:, (2 * hd) * w:(2 * hd + 1) * w]
        a2 = a_ref[:, (2 * hd + 1) * w:(2 * hd + 2) * w]
        d = a1 - lam * a2
        d = d * lax.rsqrt(jnp.mean(d * d, axis=-1, keepdims=True) + RMS_EPS) * g_ref[...]
        o_ref[:, hd * w:(hd + 1) * w] = (d * (1.0 - lam_init)).astype(o_ref.dtype)


def _diff_combine(a, diff_lambda, g_subln, lam_init, bm=512):
    M = a.shape[0]
    bm = min(bm, M)
    w = 2 * DIFF_DIM
    return pl.pallas_call(
        functools.partial(_diff_combine_kernel, lam_init=lam_init),
        out_shape=jax.ShapeDtypeStruct((M, H_DIFF * w), BF16),
        grid=(M // bm,),
        in_specs=[pl.BlockSpec((bm, 2 * H_DIFF * w), lambda i: (i, 0)),
                  pl.BlockSpec((4, DIFF_DIM), lambda i: (0, 0)),
                  pl.BlockSpec((1, w), lambda i: (0, 0))],
        out_specs=pl.BlockSpec((bm, H_DIFF * w), lambda i: (i, 0)),
        compiler_params=_cparams(("parallel",)),
        name="diff_combine",
    )(a, diff_lambda.astype(F32), g_subln.reshape(1, w).astype(F32))


PEER_SUB = 256


def _top_values(s, n):
    vals = []
    work = s
    for r in range(n):
        m = jnp.max(work, axis=0, keepdims=True)
        vals.append(m)
        if r + 1 < n:
            work = jnp.where(work >= m, -jnp.inf, work)
    return vals


def _peer_stats_kernel(q_ref, keys_ref, s2_ref, c_ref, tau_ref):
    nt = (((1,), (1,)), ((), ()))
    tb = q_ref.shape[0]
    k1 = keys_ref[0, 0]
    k2 = keys_ref[0, 1]
    for t0 in range(0, tb, PEER_SUB):
        q1 = q_ref[t0:t0 + PEER_SUB, 0:PEER_HALF]
        q2 = q_ref[t0:t0 + PEER_SUB, PEER_HALF:2 * PEER_HALF]
        s1 = lax.dot_general(k1, q1, nt, preferred_element_type=F32)
        s2 = lax.dot_general(k2, q2, nt, preferred_element_type=F32)
        sv1 = _top_values(s1, PEER_TOPK)
        sv2 = jnp.concatenate(_top_values(s2, PEER_TOPK), axis=0)
        rank = lax.broadcasted_iota(jnp.int32, (8, PEER_SUB), 0)
        sv2_lo = sv2[0:8]
        parts = [sv1[0] + sv2, sv1[1] + sv2_lo]
        for a in range(2, PEER_TOPK):
            parts.append(jnp.where(rank < PEER_TOPK // (a + 1), sv1[a] + sv2_lo, -jnp.inf))
        cand = jnp.concatenate(parts, axis=0)
        top = _top_values(cand, PEER_TOPK + 1)
        cmax = top[0]
        tau = 0.5 * (top[PEER_TOPK - 1] + top[PEER_TOPK])
        z = jnp.sum(jnp.where(cand >= tau, jnp.exp(cand - cmax), 0.0), axis=0, keepdims=True)
        lse = cmax + jnp.log(z)
        s2_ref[0, :, t0:t0 + PEER_SUB] = s2 * LOG2E
        c_ref[0, :, t0:t0 + PEER_SUB] = (s1 - lse) * LOG2E
        tau_ref[0, :, t0:t0 + PEER_SUB] = (tau - lse) * LOG2E


def _peer_stats(q, keys, tb=1024):
    T = q.shape[0]
    tb = min(tb, T)
    assert T % tb == 0 and tb % PEER_SUB == 0
    big = jax.ShapeDtypeStruct((PEER_HEADS, N_KEYS, T), F32)
    return pl.pallas_call(
        _peer_stats_kernel,
        out_shape=(big, big, jax.ShapeDtypeStruct((PEER_HEADS, 1, T), F32)),
        grid=(T // tb, PEER_HEADS),
        in_specs=[pl.BlockSpec((tb, 2 * PEER_HALF), lambda i, h: (i, h)),
                  pl.BlockSpec((1, 2, N_KEYS, PEER_HALF), lambda i, h: (h, 0, 0, 0))],
        out_specs=(pl.BlockSpec((1, N_KEYS, tb), lambda i, h: (h, 0, i)),
                   pl.BlockSpec((1, N_KEYS, tb), lambda i, h: (h, 0, i)),
                   pl.BlockSpec((1, 1, tb), lambda i, h: (h, 0, i))),
        compiler_params=_cparams(("parallel", "parallel")),
        name="peer_stats",
    )(q, keys)


GELU_K = math.sqrt(2.0 / math.pi)
GATE_ROWS = 32
MIX_TOK = 128
MIX_A_UNITS = 2
MIX_C_UNITS = 16


def _peer_mix_body(xT_ref, u_ref, vT_ref, s2_ref, c_ref, tau_ref, o_ref,
                   h_wr, h_rd, w_wr, w_rd, ni):
    tb = xT_ref.shape[1]
    d = vT_ref.shape[0]

    def gate_sub(il, t0, k0):
        ts = slice(t0, t0 + MIX_TOK)
        ks = slice(k0, k0 + GATE_ROWS)
        rows = slice(il * N_KEYS + k0, il * N_KEYS + k0 + GATE_ROWS)
        g = None
        for h in range(PEER_HEADS):
            z = s2_ref[h, ks, ts] + c_ref[h, 0, il:il + 1, ts]
            e = jnp.where(z >= tau_ref[h, :, ts], jnp.exp2(z), 0.0)
            g = e if g is None else g + e
        x = h_rd[rows, ts]
        inner = x * ((x * x) * (0.044715 * GELU_K) + GELU_K)
        hx = 0.5 * x
        act = hx + hx * jnp.tanh(inner)
        w_wr[rows, ts] = (g * act).astype(BF16)

    ec = u_ref.shape[0]
    a_part = ec // MIX_A_UNITS
    c_part = d // MIX_C_UNITS
    for t0 in range(0, tb, 2 * MIX_TOK):
        tq = slice(t0, t0 + 2 * MIX_TOK)

        def unit_a(j):
            h_wr[j * a_part:(j + 1) * a_part, tq] = jnp.dot(
                u_ref[j * a_part:(j + 1) * a_part, :], xT_ref[:, tq], preferred_element_type=F32)

        def unit_c(j):
            o_ref[j * c_part:(j + 1) * c_part, tq] += jnp.dot(
                vT_ref[j * c_part:(j + 1) * c_part, :], w_rd[:, tq], preferred_element_type=F32)

        mxu = []
        c_per_a = MIX_C_UNITS // MIX_A_UNITS
        for j in range(MIX_A_UNITS):
            mxu.append(functools.partial(unit_a, j))
            mxu += [functools.partial(unit_c, j * c_per_a + i) for i in range(c_per_a)]
        vpu = [functools.partial(gate_sub, r, t0 + dt, k0)
               for r in range(ni) for dt in (0, MIX_TOK) for k0 in range(0, N_KEYS, GATE_ROWS)]
        done = 0
        for i, unit in enumerate(mxu):
            unit()
            upto = ((i + 1) * len(vpu)) // len(mxu)
            for v in vpu[done:upto]:
                v()
            done = upto


def _peer_mix_kernel(xT_ref, u_ref, vT_ref, s2_ref, c_ref, tau_ref, o_ref, h0, h1, w0, w1, *, ni):
    c = pl.program_id(1)

    @pl.when(c == 0)
    def _():
        o_ref[...] = jnp.zeros(o_ref.shape, F32)
        h1[...] = jnp.zeros(h1.shape, F32)
        w0[...] = jnp.zeros(w0.shape, BF16)

    args = (xT_ref, u_ref, vT_ref, s2_ref, c_ref, tau_ref, o_ref)

    @pl.when(lax.rem(c, 2) == 0)
    def _():
        _peer_mix_body(*args, h0, h1, w1, w0, ni)

    @pl.when(lax.rem(c, 2) == 1)
    def _():
        _peer_mix_body(*args, h1, h0, w0, w1, ni)


def _peer_mix(xT, u, vT, s2, c, tau, *, tb=512, ec=512):
    D, T = xT.shape
    E = u.shape[0]
    tb = min(tb, T)
    assert T % tb == 0 and E % ec == 0 and ec % N_KEYS == 0 and tb % (2 * MIX_TOK) == 0
    ni = ec // N_KEYS
    nc = E // ec
    once = pl.Buffered(1)
    return pl.pallas_call(
        functools.partial(_peer_mix_kernel, ni=ni),
        out_shape=jax.ShapeDtypeStruct((D, T), F32),
        grid=(T // tb, nc + 2),
        in_specs=[pl.BlockSpec((D, tb), lambda i, e: (0, i), pipeline_mode=once),
                  pl.BlockSpec((ec, D), lambda i, e: (jnp.minimum(e, nc - 1), 0)),
                  pl.BlockSpec((D, ec), lambda i, e: (0, jnp.clip(e - 2, 0, nc - 1))),
                  pl.BlockSpec((PEER_HEADS, N_KEYS, tb), lambda i, e: (0, 0, i), pipeline_mode=once),
                  pl.BlockSpec((PEER_HEADS, 1, ni, tb), lambda i, e: (0, jnp.clip(e - 1, 0, nc - 1), 0, i)),
                  pl.BlockSpec((PEER_HEADS, 1, tb), lambda i, e: (0, 0, i), pipeline_mode=once)],
        out_specs=pl.BlockSpec((D, tb), lambda i, e: (0, i)),
        scratch_shapes=[pltpu.VMEM((ec, tb), F32), pltpu.VMEM((ec, tb), F32),
                        pltpu.VMEM((ec, tb), BF16), pltpu.VMEM((ec, tb), BF16)],
        compiler_params=_cparams(("parallel", "arbitrary")),
        name="peer_mix",
    )(xT, u, vT, s2, c.reshape(PEER_HEADS, nc, ni, T), tau)


def _rope_tables(seq, d):
    inv = ROPE_THETA ** (-jnp.arange(0, d, 2, dtype=F32) / d)
    ang = jnp.arange(seq, dtype=F32)[:, None] * inv[None, :]
    cos, sin = jnp.cos(ang), jnp.sin(ang)
    half = d // 2
    pad = jnp.zeros((seq, LANES - d), F32)
    t0 = jnp.concatenate([cos, cos, pad], axis=1)
    if d == LANES:
        return (half,), (t0, jnp.concatenate([-sin, sin], axis=1))
    zh = jnp.zeros((seq, half), F32)
    t1 = jnp.concatenate([-sin, zh, pad], axis=1)
    t2 = jnp.concatenate([zh, sin, pad], axis=1)
    return (LANES - half, half), (t0, t1, t2)


def _band_bias_tiles(rel_bias, bq, bk):
    assert bq == bk
    n = bq
    nd = (LEFT_CHUNKS * CHUNK + bq - 1) // bk + 1
    m = np.arange(2 * n)
    pats = []
    for dlt in range(nd):
        rel = np.where(m < n, dlt * n - m, dlt * n + 2 * n - m)
        ridx = np.clip(rel, -REL_MAX, REL_MAX) + REL_MAX
        pats.append(jnp.take(rel_bias.astype(F32), jnp.asarray(ridx), axis=1))
    pat = jnp.stack(pats, axis=1)[:, :, None, :]
    nh = pat.shape[0]

    def toeplitz_kernel(p_ref, o_ref):
        rows = jnp.broadcast_to(p_ref[0, 0], (n, 2 * n))
        o_ref[0, 0] = pltpu.roll(rows, 0, 1, stride=1, stride_axis=0)[:, :n]

    return pl.pallas_call(
        toeplitz_kernel,
        out_shape=jax.ShapeDtypeStruct((nh, nd, n, n), F32),
        grid=(nh, nd),
        in_specs=[pl.BlockSpec((1, 1, 1, 2 * n), lambda h, j: (h, j, 0, 0))],
        out_specs=pl.BlockSpec((1, 1, n, n), lambda h, j: (h, j, 0, 0)),
        compiler_params=_cparams(("parallel", "parallel")),
        name="band_bias",
    )(pat)


def _decay_columns(cum):
    def top_bits(a):
        bits = lax.bitcast_convert_type(a, jnp.uint32) & jnp.uint32(0xFFFF0000)
        return lax.bitcast_convert_type(bits, F32)

    hi_f = top_bits(cum)
    r1 = cum - hi_f
    mid_f = top_bits(r1)
    pieces = [p[..., None] for p in (hi_f, mid_f, r1 - mid_f)]
    lane = lax.broadcasted_iota(jnp.int32, cum.shape + (HEAD_DIM,), 2)

    def columns(first3, next3):
        out = jnp.zeros(cum.shape + (HEAD_DIM,), F32)
        for i in range(3):
            out = jnp.where(lane == i, first3[i], out)
            out = jnp.where(lane == 3 + i, next3[i], out)
        return out.astype(BF16).reshape(cum.shape[0], -1)

    ones = [jnp.float32(1.0)] * 3
    return columns(pieces, ones), columns(ones, [-p for p in pieces])


def _even_layer(xb, w_in, b_forget, rel_bias, w_out, batch, seq):
    hd = HEAD_DIM
    scale = hd ** -0.5 * LOG2E
    n = H_FOX * hd
    fq, fk, fv = w_in[:, 0:n], w_in[:, n:2 * n], w_in[:, 2 * n:3 * n]
    wf = w_in[:, 3 * n:3 * n + H_FOX]
    o = 3 * n + H_FOX
    m = H_CHK * hd
    cq, ck, cv = w_in[:, o:o + m], w_in[:, o + m:o + 2 * m], w_in[:, o + 2 * m:o + 3 * m]
    w_main = jnp.concatenate([fq, cq, fk, fv, ck, cv], axis=1).astype(BF16)
    col_scale = jnp.concatenate([jnp.full((n + m,), scale, F32), jnp.ones((2 * n + 2 * m,), F32)])
    proj = _mm(xb, w_main, out_dtype=BF16, col_scale=col_scale)

    wf_pad = jnp.pad(wf, ((0, 0), (0, LANES - H_FOX))).astype(BF16)
    logits = _mm(xb, wf_pad, out_dtype=F32)
    bias_row = jnp.pad(b_forget.astype(F32), (0, LANES - H_FOX)).reshape(1, LANES)
    cum = _gate_scan(logits, bias_row, batch, seq)
    qx, kx = _decay_columns(cum[:, :H_FOX] * LOG2E)

    nh = n // hd
    o_fox = _attention(proj, proj, proj, batch=batch, seq=seq, heads=H_FOX,
                       q_off=0, k_off=2 * nh, v_off=3 * nh, dv=hd, v_group=1, mode="causal",
                       bq=512, bk=512, out_dtype=BF16, qx=qx, kx=kx)
    bqb = min(256, seq)
    bias = _band_bias_tiles(rel_bias * LOG2E, bqb, bqb)
    o_chk = _attention(proj, proj, proj, batch=batch, seq=seq, heads=H_CHK,
                       q_off=nh, k_off=4 * nh, v_off=5 * nh, dv=hd, v_group=1, mode="band",
                       bq=bqb, bk=bqb, out_dtype=BF16, bias=bias)
    return o_fox, o_chk, w_out.astype(BF16)


def _odd_layer(xb, w_in, g_q_lora, g_kv_lora, w_uq, w_ukv, diff_lambda, g_subln, w_out,
               layer_idx, batch, seq):
    o1 = Q_LORA + KV_LORA
    o2 = o1 + MLA_ROPE
    nd = H_DIFF * 2 * DIFF_DIM
    w_lora = w_in[:, :o1].astype(BF16)
    w_kpe = jnp.pad(w_in[:, o1:o2], ((0, 0), (0, LANES - MLA_ROPE))).astype(BF16)
    w_dqk = w_in[:, o2:o2 + 2 * nd].astype(BF16)
    w_dv = w_in[:, o2 + 2 * nd:o2 + 3 * nd].astype(BF16)

    rope128 = _rope_tables(seq, DIFF_DIM)
    rope64 = _rope_tables(seq, MLA_ROPE)

    lora = _mm(xb, w_lora, out_dtype=F32)
    kpe = _mm(xb, w_kpe, out_dtype=BF16, rope=rope64)
    dscale = jnp.concatenate([jnp.full((nd,), DIFF_DIM ** -0.5 * LOG2E, F32), jnp.ones((nd,), F32)])
    dqk = _mm(xb, w_dqk, out_dtype=BF16, col_scale=dscale, rope=rope128)
    dv = _mm(xb, w_dv, out_dtype=BF16)

    mscale = (MLA_NOPE + MLA_ROPE) ** -0.5 * LOG2E
    wq = w_uq.reshape(Q_LORA, H_MLA, MLA_NOPE + MLA_ROPE)
    wq_nope = wq[:, :, :MLA_NOPE].reshape(Q_LORA, H_MLA * MLA_NOPE).astype(BF16)
    wq_rope = jnp.pad(wq[:, :, MLA_NOPE:], ((0, 0), (0, 0), (0, LANES - MLA_ROPE)))
    wq_rope = wq_rope.reshape(Q_LORA, H_MLA * LANES).astype(BF16)
    cq = lora[:, :Q_LORA]
    ckv = lora[:, Q_LORA:]
    ms = jnp.full((H_MLA * LANES,), mscale, F32)
    q_nope = _mm(cq, wq_nope, out_dtype=BF16, gain=g_q_lora, col_scale=ms)
    q_rope = _mm(cq, wq_rope, out_dtype=BF16, gain=g_q_lora, col_scale=ms, rope=rope64)
    wkv = w_ukv.reshape(KV_LORA, H_MLA, MLA_NOPE + MLA_V)
    wkv = jnp.concatenate([wkv[:, :, :MLA_NOPE].reshape(KV_LORA, H_MLA * MLA_NOPE),
                           wkv[:, :, MLA_NOPE:].reshape(KV_LORA, H_MLA * MLA_V)], axis=1)
    kv = _mm(ckv, wkv.astype(BF16), out_dtype=BF16, gain=g_kv_lora)

    o_mla = _attention(q_nope, kv, kv, batch=batch, seq=seq, heads=H_MLA,
                       q_off=0, k_off=0, v_off=H_MLA, dv=MLA_V, v_group=1, mode="chunk",
                       bq=512, bk=512, out_dtype=BF16, qx=q_rope, kx=kpe)
    a = _attention(dqk, dqk, dv, batch=batch, seq=seq, heads=2 * H_DIFF,
                   q_off=0, k_off=2 * H_DIFF, v_off=0, dv=2 * DIFF_DIM, v_group=2, mode="chunk",
                   bq=512, bk=512, out_dtype=F32)
    lam_init = 0.8 - 0.6 * math.exp(-0.3 * layer_idx)
    o_diff = _diff_combine(a, diff_lambda, g_subln, lam_init)
    return o_mla, o_diff, w_out.astype(BF16)


def _peer_layer(xb, xbT, w_query, sub_keys, u_tab, v_tab):
    q = _mm(xb, w_query.astype(BF16), out_dtype=BF16)
    s2, c, tau = _peer_stats(q, sub_keys.astype(BF16))
    return _peer_mix(xbT, u_tab.astype(BF16), v_tab.T.astype(BF16), s2, c, tau)


def kernel(x, w_in_even, b_forget, rel_bias, w_out_even, w_in_odd, g_q_lora, g_kv_lora, w_uq, w_ukv, diff_lambda, g_subln, w_out_odd, peer_w_query, peer_sub_keys, peer_u, peer_v, ln_mix_g, ln_mix_b, ln_ffn_g, ln_ffn_b):
    batch, seq, d = x.shape
    depth = ln_mix_g.shape[0]
    alpha = (2 * depth) ** 0.25
    xf = x.reshape(batch * seq, d).astype(F32)
    xb = xf.astype(BF16)
    for l in range(depth):
        i = l // 2
        if l % 2 == 0:
            oa, ob, wo = _even_layer(xb, w_in_even[i], b_forget[i], rel_bias[i], w_out_even[i],
                                     batch, seq)
        else:
            oa, ob, wo = _odd_layer(xb, w_in_odd[i], g_q_lora[i], g_kv_lora[i], w_uq[i], w_ukv[i],
                                    diff_lambda[i], g_subln[i], w_out_odd[i], l, batch, seq)
        xf, xb, xbT = _proj_res_ln(oa, ob, wo, xf, ln_mix_g[l], ln_mix_b[l], alpha)
        yT = _peer_layer(xb, xbT, peer_w_query[l], peer_sub_keys[l], peer_u[l], peer_v[l])
        xf, xb = _res_ln(xf, yT, ln_ffn_g[l], ln_ffn_b[l], alpha, y_transposed=True)
    return xf.reshape(batch, seq, d)
```

```python
import functools
import math

import numpy as np
import jax
import jax.numpy as jnp
from jax import lax
from jax.experimental import pallas as pl
from jax.experimental.pallas import tpu as pltpu

F32 = jnp.float32
BF16 = jnp.bfloat16

CHUNK = 64
HEAD_DIM = 128
ROPE_THETA = 10000.0
LN_EPS = 1e-5
RMS_EPS = 1e-6
NEG_INF = -1e30
H_FOX = 8
H_CHK = 8
LEFT_CHUNKS = 8
REL_MAX = 128
H_MLA = 8
Q_LORA = 512
KV_LORA = 512
MLA_NOPE = 128
MLA_ROPE = 64
MLA_V = 128
H_DIFF = 4
DIFF_DIM = 128
N_KEYS = 128
PEER_HEADS = 8
PEER_TOPK = 16
PEER_HALF = 128

LOG2E = math.log2(math.e)
LANES = 128
VMEM_LIMIT = 56 * 1024 * 1024


def _cparams(sem):
    return pltpu.CompilerParams(dimension_semantics=sem, vmem_limit_bytes=VMEM_LIMIT)


def _mm_kernel(*refs, has_gain, has_scale, rope_shifts, n_tabs):
    it = iter(refs)
    x_ref = next(it)
    w_ref = next(it)
    g_ref = next(it) if has_gain else None
    sc_ref = next(it) if has_scale else None
    tabs = [next(it) for _ in range(n_tabs)]
    o_ref = next(it)

    x = x_ref[...]
    if has_gain:
        xf = x.astype(F32)
        xf = xf * lax.rsqrt(jnp.mean(xf * xf, axis=-1, keepdims=True) + RMS_EPS) * g_ref[...]
        x = xf
    x = x.astype(BF16)
    acc = jnp.dot(x, w_ref[...], preferred_element_type=F32)
    if has_scale:
        acc = acc * sc_ref[...]
    if n_tabs:
        t = [r[...] for r in tabs]
        for j in range(acc.shape[1] // LANES):
            y = acc[:, j * LANES:(j + 1) * LANES]
            out = y * t[0]
            for k, s in enumerate(rope_shifts):
                out = out + pltpu.roll(y, s, 1) * t[k + 1]
            o_ref[:, j * LANES:(j + 1) * LANES] = out.astype(o_ref.dtype)
    else:
        o_ref[...] = acc.astype(o_ref.dtype)


def _mm(x, w, *, out_dtype, gain=None, col_scale=None, rope=None, bm=1024, bn=1024):
    M, K = x.shape
    N = w.shape[1]
    bm = min(bm, M)
    bn = min(bn, N)
    if rope is not None:
        bm = min(bm, rope[1][0].shape[0])
    assert M % bm == 0 and N % bn == 0
    in_specs = [pl.BlockSpec((bm, K), lambda i, j: (i, 0)),
                pl.BlockSpec((K, bn), lambda i, j: (0, j))]
    args = [x, w]
    if gain is not None:
        in_specs.append(pl.BlockSpec((1, K), lambda i, j: (0, 0)))
        args.append(gain.reshape(1, K).astype(F32))
    if col_scale is not None:
        in_specs.append(pl.BlockSpec((1, bn), lambda i, j: (0, j)))
        args.append(col_scale.reshape(1, N).astype(F32))
    shifts, tabs = ((), ())
    if rope is not None:
        shifts, tabs = rope
        seq = tabs[0].shape[0]
        assert seq % bm == 0
        nsb = seq // bm
        for t in tabs:
            in_specs.append(pl.BlockSpec((bm, LANES), lambda i, j: (i % nsb, 0)))
            args.append(t)
    kern = functools.partial(_mm_kernel, has_gain=gain is not None,
                             has_scale=col_scale is not None,
                             rope_shifts=tuple(shifts), n_tabs=len(tabs))
    return pl.pallas_call(
        kern,
        out_shape=jax.ShapeDtypeStruct((M, N), out_dtype),
        grid=(M // bm, N // bn),
        in_specs=in_specs,
        out_specs=pl.BlockSpec((bm, bn), lambda i, j: (i, j)),
        compiler_params=_cparams(("parallel", "parallel")),
        name="mm",
    )(*args)


def _res_ln_kernel(x_ref, y_ref, g_ref, b_ref, o_ref, ob_ref, obt_ref, *, alpha, y_transposed):
    y = y_ref[...]
    if y_transposed:
        y = y.T
    z = alpha * x_ref[...] + y
    mu = jnp.mean(z, axis=-1, keepdims=True)
    zc = z - mu
    var = jnp.mean(zc * zc, axis=-1, keepdims=True)
    out = zc * lax.rsqrt(var + LN_EPS) * g_ref[...] + b_ref[...]
    o_ref[...] = out
    ob_ref[...] = out.astype(BF16)
    if obt_ref is not None:
        obt_ref[...] = out.T.astype(BF16)


def _res_ln(x, y, g, b, alpha, *, y_transposed=False, emit_transposed=False, bm=512):
    M, D = x.shape
    bm = min(bm, M)
    row = pl.BlockSpec((bm, D), lambda i: (i, 0))
    col = pl.BlockSpec((D, bm), lambda i: (0, i))
    vec = pl.BlockSpec((1, D), lambda i: (0, 0))
    out_shape = [jax.ShapeDtypeStruct((M, D), F32), jax.ShapeDtypeStruct((M, D), BF16)]
    out_specs = [row, row]
    if emit_transposed:
        out_shape.append(jax.ShapeDtypeStruct((D, M), BF16))
        out_specs.append(col)
        kern = functools.partial(_res_ln_kernel, alpha=alpha, y_transposed=y_transposed)
    else:
        kern = functools.partial(_res_ln_kernel, obt_ref=None, alpha=alpha, y_transposed=y_transposed)
    return pl.pallas_call(
        kern,
        out_shape=tuple(out_shape),
        grid=(M // bm,),
        in_specs=[row, col if y_transposed else row, vec, vec],
        out_specs=tuple(out_specs),
        compiler_params=_cparams(("parallel",)),
        name="res_ln",
    )(x, y, g.reshape(1, D), b.reshape(1, D))


def _proj_res_ln_kernel(a_ref, b_ref, w_ref, x_ref, g_ref, be_ref, o_ref, ob_ref, obt_ref, *, alpha):
    ka = a_ref.shape[1]
    y = (jnp.dot(a_ref[...], w_ref[0:ka, :], preferred_element_type=F32)
         + jnp.dot(b_ref[...], w_ref[ka:, :], preferred_element_type=F32))
    z = alpha * x_ref[...] + y
    mu = jnp.mean(z, axis=-1, keepdims=True)
    zc = z - mu
    var = jnp.mean(zc * zc, axis=-1, keepdims=True)
    out = zc * lax.rsqrt(var + LN_EPS) * g_ref[...] + be_ref[...]
    o_ref[...] = out
    ob_ref[...] = out.astype(BF16)
    obt_ref[...] = out.T.astype(BF16)


def _proj_res_ln(a, b2, w, x, g, b, alpha, bm=512):
    M, D = x.shape
    ka, kb = a.shape[1], b2.shape[1]
    bm = min(bm, M)
    row = pl.BlockSpec((bm, D), lambda i: (i, 0))
    vec = pl.BlockSpec((1, D), lambda i: (0, 0))
    return pl.pallas_call(
        functools.partial(_proj_res_ln_kernel, alpha=alpha),
        out_shape=(jax.ShapeDtypeStruct((M, D), F32), jax.ShapeDtypeStruct((M, D), BF16),
                   jax.ShapeDtypeStruct((D, M), BF16)),
        grid=(M // bm,),
        in_specs=[pl.BlockSpec((bm, ka), lambda i: (i, 0)),
                  pl.BlockSpec((bm, kb), lambda i: (i, 0)),
                  pl.BlockSpec((ka + kb, D), lambda i: (0, 0), pipeline_mode=pl.Buffered(1)),
                  row, vec, vec],
        out_specs=(row, row, pl.BlockSpec((D, bm), lambda i: (0, i))),
        compiler_params=_cparams(("parallel",)),
        name="proj_res_ln",
    )(a, b2, w, x, g.reshape(1, D), b.reshape(1, D))


SCAN_BLOCK = 256


def _gate_scan_kernel(z_ref, b_ref, o_ref):
    seq = z_ref.shape[0]
    r = lax.broadcasted_iota(jnp.int32, (SCAN_BLOCK, SCAN_BLOCK), 0)
    c = lax.broadcasted_iota(jnp.int32, (SCAN_BLOCK, SCAN_BLOCK), 1)
    tri = jnp.where(c <= r, 1.0, 0.0).astype(F32)
    carry = jnp.zeros((1, LANES), F32)
    for i in range(seq // SCAN_BLOCK):
        z = z_ref[i * SCAN_BLOCK:(i + 1) * SCAN_BLOCK, :] + b_ref[...]
        logf = jnp.minimum(z, 0.0) - jnp.log(1.0 + jnp.exp(-jnp.abs(z)))
        cs = jnp.dot(tri, logf, preferred_element_type=F32,
                     precision=lax.Precision.HIGHEST) + carry
        o_ref[i * SCAN_BLOCK:(i + 1) * SCAN_BLOCK, :] = cs
        carry = cs[SCAN_BLOCK - 1:SCAN_BLOCK, :]


def _gate_scan(z, bias_row, batch, seq):
    assert seq % SCAN_BLOCK == 0
    return pl.pallas_call(
        _gate_scan_kernel,
        out_shape=jax.ShapeDtypeStruct((batch * seq, LANES), F32),
        grid=(batch,),
        in_specs=[pl.BlockSpec((seq, LANES), lambda b: (b, 0)),
                  pl.BlockSpec((1, LANES), lambda b: (0, 0))],
        out_specs=pl.BlockSpec((seq, LANES), lambda b: (b, 0)),
        compiler_params=_cparams(("parallel",)),
        name="gate_scan",
    )(z, bias_row)


def _attn_kernel(*refs, mode, bq, bk, heads_per_step, dv, v_group, rsub, has_x, x_shared,
                 has_bias, ones_col):
    qt, kt, ft, lt, mt = refs[:5]
    it = iter(refs[5:])
    q_ref = next(it)
    k_ref = next(it)
    v_ref = next(it)
    qx_ref = next(it) if has_x else None
    kx_ref = next(it) if has_x else None
    bias_ref = next(it) if has_bias else None
    o_ref = next(it)
    m_sc = next(it)
    acc_sc = next(it)
    l_sc = None if ones_col else next(it)

    p = pl.program_id(2)
    qi = qt[p]
    ki = kt[p]
    accw = acc_sc.shape[2]

    @pl.when(ft[p] == 1)
    def _():
        m_sc[...] = jnp.full(m_sc.shape, NEG_INF, F32)
        acc_sc[...] = jnp.zeros(acc_sc.shape, F32)
        if l_sc is not None:
            l_sc[...] = jnp.zeros(l_sc.shape, F32)

    nt = (((1,), (1,)), ((), ()))

    def lanes(x, width):
        return x if width == LANES else jnp.concatenate([x] * (width // LANES), axis=1)

    def step(masked):
        for g in range(heads_per_step):
            hs = slice(g * HEAD_DIM, (g + 1) * HEAD_DIM)
            vs = slice((g // v_group) * dv, (g // v_group + 1) * dv)
            kk = k_ref[:, hs]
            if has_x:
                kk = jnp.concatenate([kk, kx_ref[...] if x_shared else kx_ref[:, hs]], axis=1)
            vv = v_ref[:, vs]
            if ones_col:
                vv = jnp.concatenate([vv, jnp.ones((bk, LANES), BF16)], axis=1)
            for r0 in range(0, bq, rsub):
                rs = slice(r0, r0 + rsub)
                qq = q_ref[rs, hs]
                if has_x:
                    qq = jnp.concatenate([qq, qx_ref[rs, hs]], axis=1)
                s = lax.dot_general(qq, kk, nt, preferred_element_type=F32)
                if has_bias:
                    s = s + bias_ref[g, 0, rs, :]
                if masked:
                    rows = qi * bq + r0 + lax.broadcasted_iota(jnp.int32, (rsub, bk), 0)
                    cols = ki * bk + lax.broadcasted_iota(jnp.int32, (rsub, bk), 1)
                    if mode == "causal":
                        allowed = cols <= rows
                    else:
                        qc = lax.shift_right_logical(rows, 6)
                        kc = lax.shift_right_logical(cols, 6)
                        if mode == "chunk":
                            allowed = kc <= qc
                        else:
                            dist = qc - kc
                            allowed = dist * (LEFT_CHUNKS - dist) >= 0
                    s = jnp.where(allowed, s, NEG_INF)
                m_prev = m_sc[g, rs, :]
                m_new = jnp.maximum(m_prev, jnp.max(s, axis=1, keepdims=True))
                alpha = jnp.exp2(m_prev - m_new)
                pr = jnp.exp2(s - lanes(m_new, bk))
                pv = jnp.dot(pr.astype(BF16), vv, preferred_element_type=F32)
                acc_sc[g, rs, :] = lanes(alpha, accw) * acc_sc[g, rs, :] + pv
                if l_sc is not None:
                    l_sc[g, rs, :] = alpha * l_sc[g, rs, :] + jnp.sum(pr, axis=1, keepdims=True)
                m_sc[g, rs, :] = m_new

    @pl.when(mt[p] == 1)
    def _():
        step(True)

    @pl.when(mt[p] == 0)
    def _():
        step(False)

    @pl.when(lt[p] == 1)
    def _():
        for g in range(heads_per_step):
            if ones_col:
                out = acc_sc[g, :, 0:dv] / acc_sc[g, :, dv:dv + LANES]
            else:
                out = acc_sc[g] / lanes(l_sc[g], dv)
            o_ref[:, g * dv:(g + 1) * dv] = out.astype(o_ref.dtype)


def _pair_tables(nq, mode, bq, bk):
    qs, ks, ms = [], [], []
    for qi in range(nq):
        if mode == "band":
            lo = max(0, (qi * bq - LEFT_CHUNKS * CHUNK) // bk)
        else:
            lo = 0
        hi = ((qi + 1) * bq - 1) // bk
        for ki in range(lo, hi + 1):
            qs.append(qi)
            ks.append(ki)
            q_lo, q_hi = qi * bq, (qi + 1) * bq - 1
            k_lo, k_hi = ki * bk, (ki + 1) * bk - 1
            if mode == "causal":
                full = k_hi <= q_lo
            elif mode == "chunk":
                full = k_hi // CHUNK <= q_lo // CHUNK
            else:
                full = (k_hi // CHUNK <= q_lo // CHUNK
                        and k_lo // CHUNK + LEFT_CHUNKS >= q_hi // CHUNK)
            ms.append(0 if full else 1)
    n = len(qs)
    first = [1 if (i == 0 or qs[i - 1] != qs[i]) else 0 for i in range(n)]
    last = [1 if (i == n - 1 or qs[i + 1] != qs[i]) else 0 for i in range(n)]
    as_i32 = lambda a: jnp.asarray(np.array(a, dtype=np.int32))
    return as_i32(qs), as_i32(ks), as_i32(first), as_i32(last), as_i32(ms), n


def _attention(q, k, v, *, batch, seq, heads, q_off, k_off, v_off, dv, v_group, mode,
               bq, bk, out_dtype, heads_per_step=4, rsub=256,
               qx=None, kx=None, qx_off=0, bias=None):
    bq = min(bq, seq)
    bk = min(bk, seq)
    rsub = min(rsub, bq)
    nq = seq // bq
    nk = seq // bk
    G = heads_per_step
    assert heads % G == 0 and G % v_group == 0 and q_off % G == 0 and k_off % G == 0
    assert (v_off * v_group) % G == 0 and qx_off % G == 0
    vw = dv * G // v_group
    qt, kt, ft, lt, mt, npairs = _pair_tables(nq, mode, bq, bk)
    has_x = qx is not None
    x_shared = has_x and kx.shape[1] == HEAD_DIM
    has_bias = bias is not None
    ones_col = dv == LANES
    W = G * HEAD_DIM

    def im(off_blocks, row_tab, nblk):
        def f(b, h, p, qt, kt, ft, lt, mt):
            t = qt if row_tab == "q" else kt
            return (b * nblk + t[p], off_blocks + h)
        return f

    in_specs = [
        pl.BlockSpec((bq, W), im(q_off // G, "q", nq)),
        pl.BlockSpec((bk, W), im(k_off // G, "k", nk)),
        pl.BlockSpec((bk, vw), im(v_off * v_group // G, "k", nk)),
    ]
    args = [q, k, v]
    if has_x:
        in_specs.append(pl.BlockSpec((bq, W), im(qx_off // G, "q", nq)))
        if x_shared:
            in_specs.append(pl.BlockSpec(
                (bk, HEAD_DIM), lambda b, h, p, qt, kt, ft, lt, mt: (b * nk + kt[p], 0)))
        else:
            in_specs.append(pl.BlockSpec((bk, W), im(0, "k", nk)))
        args += [qx, kx]
    if has_bias:
        in_specs.append(
            pl.BlockSpec((G, 1, bq, bk), lambda b, h, p, qt, kt, ft, lt, mt: (h, qt[p] - kt[p], 0, 0)))
        args.append(bias)

    kern = functools.partial(_attn_kernel, mode=mode, bq=bq, bk=bk, heads_per_step=G, dv=dv,
                             v_group=v_group, rsub=rsub, has_x=has_x, x_shared=x_shared,
                             has_bias=has_bias, ones_col=ones_col)
    return pl.pallas_call(
        kern,
        out_shape=jax.ShapeDtypeStruct((batch * seq, heads * dv), out_dtype),
        grid_spec=pltpu.PrefetchScalarGridSpec(
            num_scalar_prefetch=5,
            grid=(batch, heads // G, npairs),
            in_specs=in_specs,
            out_specs=pl.BlockSpec((bq, G * dv),
                                   lambda b, h, p, qt, kt, ft, lt, mt: (b * nq + qt[p], h)),
            scratch_shapes=[pltpu.VMEM((G, bq, LANES), F32),
                            pltpu.VMEM((G, bq, dv + LANES if ones_col else dv), F32)]
            + ([] if ones_col else [pltpu.VMEM((G, bq, LANES), F32)]),
        ),
        compiler_params=_cparams(("parallel", "parallel", "arbitrary")),
        name="attn_" + mode,
    )(qt, kt, ft, lt, mt, *args)


def _diff_combine_kernel(a_ref, lam_ref, g_ref, o_ref, *, lam_init):
    lf = lam_ref[...]
    lam = (jnp.exp(jnp.sum(lf[0:1] * lf[1:2], axis=-1, keepdims=True))
           - jnp.exp(jnp.sum(lf[2:3] * lf[3:4], axis=-1, keepdims=True)) + lam_init)
    w = 2 * DIFF_DIM
    for hd in range(H_DIFF):
        a1 = a_ref[:, (2 * hd) * w:(2 * hd + 1) * w]
        a2 = a_ref[:, (2 * hd + 1) * w:(2 * hd + 2) * w]
        d = a1 - lam * a2
        d = d * lax.rsqrt(jnp.mean(d * d, axis=-1, keepdims=True) + RMS_EPS) * g_ref[...]
        o_ref[:, hd * w:(hd + 1) * w] = (d * (1.0 - lam_init)).astype(o_ref.dtype)


def _diff_combine(a, diff_lambda, g_subln, lam_init, bm=512):
    M = a.shape[0]
    bm = min(bm, M)
    w = 2 * DIFF_DIM
    return pl.pallas_call(
        functools.partial(_diff_combine_kernel, lam_init=lam_init),
        out_shape=jax.ShapeDtypeStruct((M, H_DIFF * w), BF16),
        grid=(M // bm,),
        in_specs=[pl.BlockSpec((bm, 2 * H_DIFF * w), lambda i: (i, 0)),
                  pl.BlockSpec((4, DIFF_DIM), lambda i: (0, 0)),
                  pl.BlockSpec((1, w), lambda i: (0, 0))],
        out_specs=pl.BlockSpec((bm, H_DIFF * w), lambda i: (i, 0)),
        compiler_params=_cparams(("parallel",)),
        name="diff_combine",
    )(a, diff_lambda.astype(F32), g_subln.reshape(1, w).astype(F32))


PEER_SUB = 256


def _top_values(s, n):
    vals = []
    work = s
    for r in range(n):
        m = jnp.max(work, axis=0, keepdims=True)
        vals.append(m)
        if r + 1 < n:
            work = jnp.where(work >= m, -jnp.inf, work)
    return vals


def _peer_stats_kernel(q_ref, keys_ref, s2_ref, c_ref, tau_ref):
    nt = (((1,), (1,)), ((), ()))
    tb = q_ref.shape[0]
    k1 = keys_ref[0, 0]
    k2 = keys_ref[0, 1]
    for t0 in range(0, tb, PEER_SUB):
        q1 = q_ref[t0:t0 + PEER_SUB, 0:PEER_HALF]
        q2 = q_ref[t0:t0 + PEER_SUB, PEER_HALF:2 * PEER_HALF]
        s1 = lax.dot_general(k1, q1, nt, preferred_element_type=F32)
        s2 = lax.dot_general(k2, q2, nt, preferred_element_type=F32)
        sv1 = _top_values(s1, PEER_TOPK)
        sv2 = jnp.concatenate(_top_values(s2, PEER_TOPK), axis=0)
        rank = lax.broadcasted_iota(jnp.int32, (8, PEER_SUB), 0)
        sv2_lo = sv2[0:8]
        parts = [sv1[0] + sv2, sv1[1] + sv2_lo]
        for a in range(2, PEER_TOPK):
            parts.append(jnp.where(rank < PEER_TOPK // (a + 1), sv1[a] + sv2_lo, -jnp.inf))
        cand = jnp.concatenate(parts, axis=0)
        top = _top_values(cand, PEER_TOPK + 1)
        cmax = top[0]
        tau = 0.5 * (top[PEER_TOPK - 1] + top[PEER_TOPK])
        z = jnp.sum(jnp.where(cand >= tau, jnp.exp(cand - cmax), 0.0), axis=0, keepdims=True)
        lse = cmax + jnp.log(z)
        s2_ref[0, :, t0:t0 + PEER_SUB] = s2 * LOG2E
        c_ref[0, :, t0:t0 + PEER_SUB] = (s1 - lse) * LOG2E
        tau_ref[0, :, t0:t0 + PEER_SUB] = (tau - lse) * LOG2E


def _peer_stats(q, keys, tb=1024):
    T = q.shape[0]
    tb = min(tb, T)
    assert T % tb == 0 and tb % PEER_SUB == 0
    big = jax.ShapeDtypeStruct((PEER_HEADS, N_KEYS, T), F32)
    return pl.pallas_call(
        _peer_stats_kernel,
        out_shape=(big, big, jax.ShapeDtypeStruct((PEER_HEADS, 1, T), F32)),
        grid=(T // tb, PEER_HEADS),
        in_specs=[pl.BlockSpec((tb, 2 * PEER_HALF), lambda i, h: (i, h)),
                  pl.BlockSpec((1, 2, N_KEYS, PEER_HALF), lambda i, h: (h, 0, 0, 0))],
        out_specs=(pl.BlockSpec((1, N_KEYS, tb), lambda i, h: (h, 0, i)),
                   pl.BlockSpec((1, N_KEYS, tb), lambda i, h: (h, 0, i)),
                   pl.BlockSpec((1, 1, tb), lambda i, h: (h, 0, i))),
        compiler_params=_cparams(("parallel", "parallel")),
        name="peer_stats",
    )(q, keys)


GELU_K = math.sqrt(2.0 / math.pi)
GATE_ROWS = 32
MIX_TOK = 128
MIX_A_UNITS = 2
MIX_C_UNITS = 16


def _peer_mix_body(xT_ref, u_ref, vT_ref, s2_ref, tau_ref, o_ref, c_ref, c_off,
                   h_wr, h_rd, w_wr, w_rd, ni):
    tb = xT_ref.shape[1]
    d = vT_ref.shape[1]

    def gate_sub(il, t0, k0):
        ts = slice(t0, t0 + MIX_TOK)
        ks = slice(k0, k0 + GATE_ROWS)
        rows = slice(il * N_KEYS + k0, il * N_KEYS + k0 + GATE_ROWS)
        g = None
        for h in range(PEER_HEADS):
            z = s2_ref[h, ks, ts] + c_ref[h, c_off + il:c_off + il + 1, ts]
            e = jnp.where(z >= tau_ref[h, :, ts], jnp.exp2(z), 0.0)
            g = e if g is None else g + e
        x = h_rd[rows, ts]
        inner = x * ((x * x) * (0.044715 * GELU_K) + GELU_K)
        hx = 0.5 * x
        act = hx + hx * jnp.tanh(inner)
        w_wr[rows, ts] = (g * act).astype(BF16)

    ec = u_ref.shape[0]
    a_part = ec // MIX_A_UNITS
    c_part = d // MIX_C_UNITS
    for t0 in range(0, tb, 2 * MIX_TOK):
        tq = slice(t0, t0 + 2 * MIX_TOK)

        def unit_a(j):
            h_wr[j * a_part:(j + 1) * a_part, tq] = jnp.dot(
                u_ref[j * a_part:(j + 1) * a_part, :], xT_ref[:, tq], preferred_element_type=F32)

        def unit_c(j):
            o_ref[j * c_part:(j + 1) * c_part, tq] += jnp.dot(
                vT_ref[0, j * c_part:(j + 1) * c_part, :], w_rd[:, tq], preferred_element_type=F32)

        mxu = []
        c_per_a = MIX_C_UNITS // MIX_A_UNITS
        for j in range(MIX_A_UNITS):
            mxu.append(functools.partial(unit_a, j))
            mxu += [functools.partial(unit_c, j * c_per_a + i) for i in range(c_per_a)]
        vpu = [functools.partial(gate_sub, r, t0 + dt, k0)
               for r in range(ni) for dt in (0, MIX_TOK) for k0 in range(0, N_KEYS, GATE_ROWS)]
        done = 0
        for i, unit in enumerate(mxu):
            unit()
            upto = ((i + 1) * len(vpu)) // len(mxu)
            for v in vpu[done:upto]:
                v()
            done = upto


def _peer_mix_kernel(xT_ref, u_ref, vT_ref, s2_ref, c_ref, tau_ref, o_ref, h0, h1, w0, w1, *, ni, nc):
    s = pl.program_id(0)

    @pl.when(s == 0)
    def _():
        h1[...] = jnp.zeros(h1.shape, F32)
        w0[...] = jnp.zeros(w0.shape, BF16)

    @pl.when(lax.rem(jnp.maximum(s - 2, 0), nc) == 0)
    def _():
        o_ref[...] = jnp.zeros(o_ref.shape, F32)

    args = (xT_ref, u_ref, vT_ref, s2_ref, tau_ref, o_ref, c_ref)

    @pl.when(lax.rem(s, 2) == 0)
    def _():
        _peer_mix_body(*args, (8 - ni) % 8, h0, h1, w1, w0, ni)

    @pl.when(lax.rem(s, 2) == 1)
    def _():
        _peer_mix_body(*args, 0, h1, h0, w0, w1, ni)


def _peer_mix(xT, u, v, s2, c, tau, *, tb=512, ec=512):
    D, T = xT.shape
    E = u.shape[0]
    tb = min(tb, T)
    assert T % tb == 0 and E % ec == 0 and ec % N_KEYS == 0 and tb % (2 * MIX_TOK) == 0
    ni = ec // N_KEYS
    nc = E // ec
    assert ni in (4, 8) and nc % 2 == 0
    steps = (T // tb) * nc
    vT = v.reshape(nc, ec, D).transpose(0, 2, 1)
    once = pl.Buffered(1)

    def at(lag):
        t = lambda s: jnp.clip(s - lag, 0, steps - 1)
        return (lambda s: t(s) // nc), (lambda s: lax.rem(t(s), nc))

    (blk_a, chunk_a), (blk_b, chunk_b), (blk_c, chunk_c) = at(0), at(1), at(2)
    return pl.pallas_call(
        functools.partial(_peer_mix_kernel, ni=ni, nc=nc),
        out_shape=jax.ShapeDtypeStruct((D, T), F32),
        grid=(steps + 2,),
        in_specs=[pl.BlockSpec((D, tb), lambda s: (0, blk_a(s)), pipeline_mode=once),
                  pl.BlockSpec((ec, D), lambda s: (chunk_a(s), 0)),
                  pl.BlockSpec((1, D, ec), lambda s: (chunk_c(s), 0, 0)),
                  pl.BlockSpec((PEER_HEADS, N_KEYS, tb), lambda s: (0, 0, blk_b(s)), pipeline_mode=once),
                  pl.BlockSpec((PEER_HEADS, 8, tb), lambda s: (0, chunk_b(s) * ni // 8, blk_b(s))),
                  pl.BlockSpec((PEER_HEADS, 1, tb), lambda s: (0, 0, blk_b(s)), pipeline_mode=once)],
        out_specs=pl.BlockSpec((D, tb), lambda s: (0, blk_c(s))),
        scratch_shapes=[pltpu.VMEM((ec, tb), F32), pltpu.VMEM((ec, tb), F32),
                        pltpu.VMEM((ec, tb), BF16), pltpu.VMEM((ec, tb), BF16)],
        compiler_params=_cparams(("arbitrary",)),
        name="peer_mix",
    )(xT, u, vT, s2, c, tau)


def _rope_tables(seq, d):
    inv = ROPE_THETA ** (-jnp.arange(0, d, 2, dtype=F32) / d)
    ang = jnp.arange(seq, dtype=F32)[:, None] * inv[None, :]
    cos, sin = jnp.cos(ang), jnp.sin(ang)
    half = d // 2
    pad = jnp.zeros((seq, LANES - d), F32)
    t0 = jnp.concatenate([cos, cos, pad], axis=1)
    if d == LANES:
        return (half,), (t0, jnp.concatenate([-sin, sin], axis=1))
    zh = jnp.zeros((seq, half), F32)
    t1 = jnp.concatenate([-sin, zh, pad], axis=1)
    t2 = jnp.concatenate([zh, sin, pad], axis=1)
    return (LANES - half, half), (t0, t1, t2)


def _band_bias_tiles(rel_bias, bq, bk):
    assert bq == bk
    n = bq
    nd = (LEFT_CHUNKS * CHUNK + bq - 1) // bk + 1
    m = np.arange(2 * n)
    pats = []
    for dlt in range(nd):
        rel = np.where(m < n, dlt * n - m, dlt * n + 2 * n - m)
        ridx = np.clip(rel, -REL_MAX, REL_MAX) + REL_MAX
        pats.append(jnp.take(rel_bias.astype(F32), jnp.asarray(ridx), axis=1))
    pat = jnp.stack(pats, axis=1)[:, :, None, :]
    nh = pat.shape[0]

    def toeplitz_kernel(p_ref, o_ref):
        rows = jnp.broadcast_to(p_ref[0, 0], (n, 2 * n))
        o_ref[0, 0] = pltpu.roll(rows, 0, 1, stride=1, stride_axis=0)[:, :n]

    return pl.pallas_call(
        toeplitz_kernel,
        out_shape=jax.ShapeDtypeStruct((nh, nd, n, n), F32),
        grid=(nh, nd),
        in_specs=[pl.BlockSpec((1, 1, 1, 2 * n), lambda h, j: (h, j, 0, 0))],
        out_specs=pl.BlockSpec((1, 1, n, n), lambda h, j: (h, j, 0, 0)),
        compiler_params=_cparams(("parallel", "parallel")),
        name="band_bias",
    )(pat)


def _decay_columns(cum):
    def top_bits(a):
        bits = lax.bitcast_convert_type(a, jnp.uint32) & jnp.uint32(0xFFFF0000)
        return lax.bitcast_convert_type(bits, F32)

    hi_f = top_bits(cum)
    r1 = cum - hi_f
    mid_f = top_bits(r1)
    pieces = [p[..., None] for p in (hi_f, mid_f, r1 - mid_f)]
    lane = lax.broadcasted_iota(jnp.int32, cum.shape + (HEAD_DIM,), 2)

    def columns(first3, next3):
        out = jnp.zeros(cum.shape + (HEAD_DIM,), F32)
        for i in range(3):
            out = jnp.where(lane == i, first3[i], out)
            out = jnp.where(lane == 3 + i, next3[i], out)
        return out.astype(BF16).reshape(cum.shape[0], -1)

    ones = [jnp.float32(1.0)] * 3
    return columns(pieces, ones), columns(ones, [-p for p in pieces])


def _even_layer(xb, w_in, b_forget, rel_bias, w_out, batch, seq):
    hd = HEAD_DIM
    scale = hd ** -0.5 * LOG2E
    n = H_FOX * hd
    fq, fk, fv = w_in[:, 0:n], w_in[:, n:2 * n], w_in[:, 2 * n:3 * n]
    wf = w_in[:, 3 * n:3 * n + H_FOX]
    o = 3 * n + H_FOX
    m = H_CHK * hd
    cq, ck, cv = w_in[:, o:o + m], w_in[:, o + m:o + 2 * m], w_in[:, o + 2 * m:o + 3 * m]
    w_main = jnp.concatenate([fq, cq, fk, fv, ck, cv], axis=1).astype(BF16)
    col_scale = jnp.concatenate([jnp.full((n + m,), scale, F32), jnp.ones((2 * n + 2 * m,), F32)])
    proj = _mm(xb, w_main, out_dtype=BF16, col_scale=col_scale)

    wf_pad = jnp.pad(wf, ((0, 0), (0, LANES - H_FOX))).astype(BF16)
    logits = _mm(xb, wf_pad, out_dtype=F32)
    bias_row = jnp.pad(b_forget.astype(F32), (0, LANES - H_FOX)).reshape(1, LANES)
    cum = _gate_scan(logits, bias_row, batch, seq)
    qx, kx = _decay_columns(cum[:, :H_FOX] * LOG2E)

    nh = n // hd
    o_fox = _attention(proj, proj, proj, batch=batch, seq=seq, heads=H_FOX,
                       q_off=0, k_off=2 * nh, v_off=3 * nh, dv=hd, v_group=1, mode="causal",
                       bq=512, bk=512, out_dtype=BF16, qx=qx, kx=kx)
    bqb = min(256, seq)
    bias = _band_bias_tiles(rel_bias * LOG2E, bqb, bqb)
    o_chk = _attention(proj, proj, proj, batch=batch, seq=seq, heads=H_CHK,
                       q_off=nh, k_off=4 * nh, v_off=5 * nh, dv=hd, v_group=1, mode="band",
                       bq=bqb, bk=bqb, out_dtype=BF16, bias=bias)
    return o_fox, o_chk, w_out.astype(BF16)


def _odd_layer(xb, w_in, g_q_lora, g_kv_lora, w_uq, w_ukv, diff_lambda, g_subln, w_out,
               layer_idx, batch, seq):
    o1 = Q_LORA + KV_LORA
    o2 = o1 + MLA_ROPE
    nd = H_DIFF * 2 * DIFF_DIM
    w_lora = w_in[:, :o1].astype(BF16)
    w_kpe = jnp.pad(w_in[:, o1:o2], ((0, 0), (0, LANES - MLA_ROPE))).astype(BF16)
    w_dqk = w_in[:, o2:o2 + 2 * nd].astype(BF16)
    w_dv = w_in[:, o2 + 2 * nd:o2 + 3 * nd].astype(BF16)

    rope128 = _rope_tables(seq, DIFF_DIM)
    rope64 = _rope_tables(seq, MLA_ROPE)

    lora = _mm(xb, w_lora, out_dtype=F32)
    kpe = _mm(xb, w_kpe, out_dtype=BF16, rope=rope64)
    dscale = jnp.concatenate([jnp.full((nd,), DIFF_DIM ** -0.5 * LOG2E, F32), jnp.ones((nd,), F32)])
    dqk = _mm(xb, w_dqk, out_dtype=BF16, col_scale=dscale, rope=rope128)
    dv = _mm(xb, w_dv, out_dtype=BF16)

    mscale = (MLA_NOPE + MLA_ROPE) ** -0.5 * LOG2E
    wq = w_uq.reshape(Q_LORA, H_MLA, MLA_NOPE + MLA_ROPE)
    wq_nope = wq[:, :, :MLA_NOPE].reshape(Q_LORA, H_MLA * MLA_NOPE).astype(BF16)
    wq_rope = jnp.pad(wq[:, :, MLA_NOPE:], ((0, 0), (0, 0), (0, LANES - MLA_ROPE)))
    wq_rope = wq_rope.reshape(Q_LORA, H_MLA * LANES).astype(BF16)
    cq = lora[:, :Q_LORA]
    ckv = lora[:, Q_LORA:]
    ms = jnp.full((H_MLA * LANES,), mscale, F32)
    q_nope = _mm(cq, wq_nope, out_dtype=BF16, gain=g_q_lora, col_scale=ms)
    q_rope = _mm(cq, wq_rope, out_dtype=BF16, gain=g_q_lora, col_scale=ms, rope=rope64)
    wkv = w_ukv.reshape(KV_LORA, H_MLA, MLA_NOPE + MLA_V)
    wkv = jnp.concatenate([wkv[:, :, :MLA_NOPE].reshape(KV_LORA, H_MLA * MLA_NOPE),
                           wkv[:, :, MLA_NOPE:].reshape(KV_LORA, H_MLA * MLA_V)], axis=1)
    kv = _mm(ckv, wkv.astype(BF16), out_dtype=BF16, gain=g_kv_lora)

    o_mla = _attention(q_nope, kv, kv, batch=batch, seq=seq, heads=H_MLA,
                       q_off=0, k_off=0, v_off=H_MLA, dv=MLA_V, v_group=1, mode="chunk",
                       bq=512, bk=512, out_dtype=BF16, qx=q_rope, kx=kpe)
    a = _attention(dqk, dqk, dv, batch=batch, seq=seq, heads=2 * H_DIFF,
                   q_off=0, k_off=2 * H_DIFF, v_off=0, dv=2 * DIFF_DIM, v_group=2, mode="chunk",
                   bq=512, bk=512, out_dtype=F32)
    lam_init = 0.8 - 0.6 * math.exp(-0.3 * layer_idx)
    o_diff = _diff_combine(a, diff_lambda, g_subln, lam_init)
    return o_mla, o_diff, w_out.astype(BF16)


def _peer_layer(xb, xbT, w_query, sub_keys, u_tab, v_tab):
    q = _mm(xb, w_query.astype(BF16), out_dtype=BF16)
    s2, c, tau = _peer_stats(q, sub_keys.astype(BF16))
    return _peer_mix(xbT, u_tab.astype(BF16), v_tab.astype(BF16), s2, c, tau)


def kernel(x, w_in_even, b_forget, rel_bias, w_out_even, w_in_odd, g_q_lora, g_kv_lora, w_uq, w_ukv, diff_lambda, g_subln, w_out_odd, peer_w_query, peer_sub_keys, peer_u, peer_v, ln_mix_g, ln_mix_b, ln_ffn_g, ln_ffn_b):
    batch, seq, d = x.shape
    depth = ln_mix_g.shape[0]
    alpha = (2 * depth) ** 0.25
    xf = x.reshape(batch * seq, d).astype(F32)
    xb = xf.astype(BF16)
    for l in range(depth):
        i = l // 2
        if l % 2 == 0:
            oa, ob, wo = _even_layer(xb, w_in_even[i], b_forget[i], rel_bias[i], w_out_even[i],
                                     batch, seq)
        else:
            oa, ob, wo = _odd_layer(xb, w_in_odd[i], g_q_lora[i], g_kv_lora[i], w_uq[i], w_ukv[i],
                                    diff_lambda[i], g_subln[i], w_out_odd[i], l, batch, seq)
        xf, xb, xbT = _proj_res_ln(oa, ob, wo, xf, ln_mix_g[l], ln_mix_b[l], alpha)
        yT = _peer_layer(xb, xbT, peer_w_query[l], peer_sub_keys[l], peer_u[l], peer_v[l])
        xf, xb = _res_ln(xf, yT, ln_ffn_g[l], ln_ffn_b[l], alpha, y_transposed=True)
    return xf.reshape(batch, seq, d)
```

```python
import functools
import math

import numpy as np
import jax
import jax.numpy as jnp
from jax import lax
from jax.experimental import pallas as pl
from jax.experimental.pallas import tpu as pltpu

F32 = jnp.float32
BF16 = jnp.bfloat16

CHUNK = 64
HEAD_DIM = 128
ROPE_THETA = 10000.0
LN_EPS = 1e-5
RMS_EPS = 1e-6
NEG_INF = -1e30
H_FOX = 8
H_CHK = 8
LEFT_CHUNKS = 8
REL_MAX = 128
H_MLA = 8
Q_LORA = 512
KV_LORA = 512
MLA_NOPE = 128
MLA_ROPE = 64
MLA_V = 128
H_DIFF = 4
DIFF_DIM = 128
N_KEYS = 128
PEER_HEADS = 8
PEER_TOPK = 16
PEER_HALF = 128

LOG2E = math.log2(math.e)
LANES = 128
VMEM_LIMIT = 56 * 1024 * 1024


def _cparams(sem):
    return pltpu.CompilerParams(dimension_semantics=sem, vmem_limit_bytes=VMEM_LIMIT)


def _mm_kernel(*refs, has_gain, has_scale, rope_shifts, n_tabs):
    it = iter(refs)
    x_ref = next(it)
    w_ref = next(it)
    g_ref = next(it) if has_gain else None
    sc_ref = next(it) if has_scale else None
    tabs = [next(it) for _ in range(n_tabs)]
    o_ref = next(it)

    x = x_ref[...]
    if has_gain:
        xf = x.astype(F32)
        xf = xf * lax.rsqrt(jnp.mean(xf * xf, axis=-1, keepdims=True) + RMS_EPS) * g_ref[...]
        x = xf
    x = x.astype(BF16)
    acc = jnp.dot(x, w_ref[...], preferred_element_type=F32)
    if has_scale:
        acc = acc * sc_ref[...]
    if n_tabs:
        t = [r[...] for r in tabs]
        for j in range(acc.shape[1] // LANES):
            y = acc[:, j * LANES:(j + 1) * LANES]
            out = y * t[0]
            for k, s in enumerate(rope_shifts):
                out = out + pltpu.roll(y, s, 1) * t[k + 1]
            o_ref[:, j * LANES:(j + 1) * LANES] = out.astype(o_ref.dtype)
    else:
        o_ref[...] = acc.astype(o_ref.dtype)


def _mm(x, w, *, out_dtype, gain=None, col_scale=None, rope=None, bm=1024, bn=1024):
    M, K = x.shape
    N = w.shape[1]
    bm = min(bm, M)
    bn = min(bn, N)
    if rope is not None:
        bm = min(bm, rope[1][0].shape[0])
    assert M % bm == 0 and N % bn == 0
    in_specs = [pl.BlockSpec((bm, K), lambda i, j: (i, 0)),
                pl.BlockSpec((K, bn), lambda i, j: (0, j))]
    args = [x, w]
    if gain is not None:
        in_specs.append(pl.BlockSpec((1, K), lambda i, j: (0, 0)))
        args.append(gain.reshape(1, K).astype(F32))
    if col_scale is not None:
        in_specs.append(pl.BlockSpec((1, bn), lambda i, j: (0, j)))
        args.append(col_scale.reshape(1, N).astype(F32))
    shifts, tabs = ((), ())
    if rope is not None:
        shifts, tabs = rope
        seq = tabs[0].shape[0]
        assert seq % bm == 0
        nsb = seq // bm
        for t in tabs:
            in_specs.append(pl.BlockSpec((bm, LANES), lambda i, j: (i % nsb, 0)))
            args.append(t)
    kern = functools.partial(_mm_kernel, has_gain=gain is not None,
                             has_scale=col_scale is not None,
                             rope_shifts=tuple(shifts), n_tabs=len(tabs))
    return pl.pallas_call(
        kern,
        out_shape=jax.ShapeDtypeStruct((M, N), out_dtype),
        grid=(M // bm, N // bn),
        in_specs=in_specs,
        out_specs=pl.BlockSpec((bm, bn), lambda i, j: (i, j)),
        compiler_params=_cparams(("parallel", "parallel")),
        name="mm",
    )(*args)


def _res_ln_kernel(x_ref, yt_ref, g_ref, b_ref, o_ref, ob_ref, *, alpha):
    z = alpha * x_ref[...] + yt_ref[...].T
    mu = jnp.mean(z, axis=-1, keepdims=True)
    zc = z - mu
    var = jnp.mean(zc * zc, axis=-1, keepdims=True)
    out = zc * lax.rsqrt(var + LN_EPS) * g_ref[...] + b_ref[...]
    o_ref[...] = out
    ob_ref[...] = out.astype(BF16)


def _res_ln(x, y_t, g, b, alpha, bm=512):
    M, D = x.shape
    bm = min(bm, M)
    row = pl.BlockSpec((bm, D), lambda i: (i, 0))
    vec = pl.BlockSpec((1, D), lambda i: (0, 0))
    return pl.pallas_call(
        functools.partial(_res_ln_kernel, alpha=alpha),
        out_shape=(jax.ShapeDtypeStruct((M, D), F32), jax.ShapeDtypeStruct((M, D), BF16)),
        grid=(M // bm,),
        in_specs=[row, pl.BlockSpec((D, bm), lambda i: (0, i)), vec, vec],
        out_specs=(row, row),
        compiler_params=_cparams(("parallel",)),
        name="res_ln",
    )(x, y_t, g.reshape(1, D), b.reshape(1, D))


def _proj_res_ln_kernel(a_ref, b_ref, w_ref, x_ref, g_ref, be_ref, o_ref, ob_ref, obt_ref, *, alpha):
    ka = a_ref.shape[1]
    y = (jnp.dot(a_ref[...], w_ref[0:ka, :], preferred_element_type=F32)
         + jnp.dot(b_ref[...], w_ref[ka:, :], preferred_element_type=F32))
    z = alpha * x_ref[...] + y
    mu = jnp.mean(z, axis=-1, keepdims=True)
    zc = z - mu
    var = jnp.mean(zc * zc, axis=-1, keepdims=True)
    out = zc * lax.rsqrt(var + LN_EPS) * g_ref[...] + be_ref[...]
    o_ref[...] = out
    ob_ref[...] = out.astype(BF16)
    obt_ref[...] = out.T.astype(BF16)


def _proj_res_ln(a, b2, w, x, g, b, alpha, bm=512):
    M, D = x.shape
    ka, kb = a.shape[1], b2.shape[1]
    bm = min(bm, M)
    row = pl.BlockSpec((bm, D), lambda i: (i, 0))
    vec = pl.BlockSpec((1, D), lambda i: (0, 0))
    return pl.pallas_call(
        functools.partial(_proj_res_ln_kernel, alpha=alpha),
        out_shape=(jax.ShapeDtypeStruct((M, D), F32), jax.ShapeDtypeStruct((M, D), BF16),
                   jax.ShapeDtypeStruct((D, M), BF16)),
        grid=(M // bm,),
        in_specs=[pl.BlockSpec((bm, ka), lambda i: (i, 0)),
                  pl.BlockSpec((bm, kb), lambda i: (i, 0)),
                  pl.BlockSpec((ka + kb, D), lambda i: (0, 0), pipeline_mode=pl.Buffered(1)),
                  row, vec, vec],
        out_specs=(row, row, pl.BlockSpec((D, bm), lambda i: (0, i))),
        compiler_params=_cparams(("parallel",)),
        name="proj_res_ln",
    )(a, b2, w, x, g.reshape(1, D), b.reshape(1, D))


SCAN_BLOCK = 256


def _gate_scan_kernel(z_ref, b_ref, o_ref):
    seq = z_ref.shape[0]
    r = lax.broadcasted_iota(jnp.int32, (SCAN_BLOCK, SCAN_BLOCK), 0)
    c = lax.broadcasted_iota(jnp.int32, (SCAN_BLOCK, SCAN_BLOCK), 1)
    tri = jnp.where(c <= r, 1.0, 0.0).astype(F32)
    carry = jnp.zeros((1, LANES), F32)
    for i in range(seq // SCAN_BLOCK):
        z = z_ref[i * SCAN_BLOCK:(i + 1) * SCAN_BLOCK, :] + b_ref[...]
        logf = jnp.minimum(z, 0.0) - jnp.log(1.0 + jnp.exp(-jnp.abs(z)))
        cs = jnp.dot(tri, logf, preferred_element_type=F32,
                     precision=lax.Precision.HIGHEST) + carry
        o_ref[i * SCAN_BLOCK:(i + 1) * SCAN_BLOCK, :] = cs
        carry = cs[SCAN_BLOCK - 1:SCAN_BLOCK, :]


def _gate_scan(z, bias_row, batch, seq):
    assert seq % SCAN_BLOCK == 0
    return pl.pallas_call(
        _gate_scan_kernel,
        out_shape=jax.ShapeDtypeStruct((batch * seq, LANES), F32),
        grid=(batch,),
        in_specs=[pl.BlockSpec((seq, LANES), lambda b: (b, 0)),
                  pl.BlockSpec((1, LANES), lambda b: (0, 0))],
        out_specs=pl.BlockSpec((seq, LANES), lambda b: (b, 0)),
        compiler_params=_cparams(("parallel",)),
        name="gate_scan",
    )(z, bias_row)


def _attn_kernel(*refs, mode, bq, bk, heads_per_step, dv, v_group, rsub, has_x, x_shared,
                 has_bias, ones_col):
    qt, kt, ft, lt, mt = refs[:5]
    it = iter(refs[5:])
    q_ref = next(it)
    k_ref = next(it)
    v_ref = next(it)
    qx_ref = next(it) if has_x else None
    kx_ref = next(it) if has_x else None
    bias_ref = next(it) if has_bias else None
    o_ref = next(it)
    m_sc = next(it)
    acc_sc = next(it)
    l_sc = None if ones_col else next(it)

    p = pl.program_id(2)
    qi = qt[p]
    ki = kt[p]
    accw = acc_sc.shape[2]

    @pl.when(ft[p] == 1)
    def _():
        m_sc[...] = jnp.full(m_sc.shape, NEG_INF, F32)
        acc_sc[...] = jnp.zeros(acc_sc.shape, F32)
        if l_sc is not None:
            l_sc[...] = jnp.zeros(l_sc.shape, F32)

    nt = (((1,), (1,)), ((), ()))

    def lanes(x, width):
        return x if width == LANES else jnp.concatenate([x] * (width // LANES), axis=1)

    def step(masked):
        for g in range(heads_per_step):
            hs = slice(g * HEAD_DIM, (g + 1) * HEAD_DIM)
            vs = slice((g // v_group) * dv, (g // v_group + 1) * dv)
            kk = k_ref[:, hs]
            if has_x:
                kk = jnp.concatenate([kk, kx_ref[...] if x_shared else kx_ref[:, hs]], axis=1)
            vv = v_ref[:, vs]
            if ones_col:
                vv = jnp.concatenate([vv, jnp.ones((bk, LANES), BF16)], axis=1)
            for r0 in range(0, bq, rsub):
                rs = slice(r0, r0 + rsub)
                qq = q_ref[rs, hs]
                if has_x:
                    qq = jnp.concatenate([qq, qx_ref[rs, hs]], axis=1)
                s = lax.dot_general(qq, kk, nt, preferred_element_type=F32)
                if has_bias:
                    s = s + bias_ref[g, 0, rs, :]
                if masked:
                    rows = qi * bq + r0 + lax.broadcasted_iota(jnp.int32, (rsub, bk), 0)
                    cols = ki * bk + lax.broadcasted_iota(jnp.int32, (rsub, bk), 1)
                    if mode == "causal":
                        allowed = cols <= rows
                    else:
                        qc = lax.shift_right_logical(rows, 6)
                        kc = lax.shift_right_logical(cols, 6)
                        if mode == "chunk":
                            allowed = kc <= qc
                        else:
                            dist = qc - kc
                            allowed = dist * (LEFT_CHUNKS - dist) >= 0
                    s = jnp.where(allowed, s, NEG_INF)
                m_prev = m_sc[g, rs, :]
                m_new = jnp.maximum(m_prev, jnp.max(s, axis=1, keepdims=True))
                alpha = jnp.exp2(m_prev - m_new)
                pr = jnp.exp2(s - lanes(m_new, bk))
                pv = jnp.dot(pr.astype(BF16), vv, preferred_element_type=F32)
                acc_sc[g, rs, :] = lanes(alpha, accw) * acc_sc[g, rs, :] + pv
                if l_sc is not None:
                    l_sc[g, rs, :] = alpha * l_sc[g, rs, :] + jnp.sum(pr, axis=1, keepdims=True)
                m_sc[g, rs, :] = m_new

    @pl.when(mt[p] == 1)
    def _():
        step(True)

    @pl.when(mt[p] == 0)
    def _():
        step(False)

    @pl.when(lt[p] == 1)
    def _():
        for g in range(heads_per_step):
            if ones_col:
                out = acc_sc[g, :, 0:dv] / acc_sc[g, :, dv:dv + LANES]
            else:
                out = acc_sc[g] / lanes(l_sc[g], dv)
            o_ref[:, g * dv:(g + 1) * dv] = out.astype(o_ref.dtype)


def _pair_tables(nq, mode, bq, bk):
    qs, ks, ms = [], [], []
    for qi in range(nq):
        if mode == "band":
            lo = max(0, (qi * bq - LEFT_CHUNKS * CHUNK) // bk)
        else:
            lo = 0
        hi = ((qi + 1) * bq - 1) // bk
        for ki in range(lo, hi + 1):
            qs.append(qi)
            ks.append(ki)
            q_lo, q_hi = qi * bq, (qi + 1) * bq - 1
            k_lo, k_hi = ki * bk, (ki + 1) * bk - 1
            if mode == "causal":
                full = k_hi <= q_lo
            elif mode == "chunk":
                full = k_hi // CHUNK <= q_lo // CHUNK
            else:
                full = (k_hi // CHUNK <= q_lo // CHUNK
                        and k_lo // CHUNK + LEFT_CHUNKS >= q_hi // CHUNK)
            ms.append(0 if full else 1)
    n = len(qs)
    first = [1 if (i == 0 or qs[i - 1] != qs[i]) else 0 for i in range(n)]
    last = [1 if (i == n - 1 or qs[i + 1] != qs[i]) else 0 for i in range(n)]
    as_i32 = lambda a: jnp.asarray(np.array(a, dtype=np.int32))
    return as_i32(qs), as_i32(ks), as_i32(first), as_i32(last), as_i32(ms), n


def _attention(q, k, v, *, batch, seq, heads, q_off, k_off, v_off, dv, v_group, mode,
               bq, bk, out_dtype, heads_per_step=8, rsub=256,
               qx=None, kx=None, qx_off=0, bias=None):
    bq = min(bq, seq)
    bk = min(bk, seq)
    rsub = min(rsub, bq)
    nq = seq // bq
    nk = seq // bk
    G = heads_per_step
    assert heads % G == 0 and G % v_group == 0 and q_off % G == 0 and k_off % G == 0
    assert (v_off * v_group) % G == 0 and qx_off % G == 0
    vw = dv * G // v_group
    qt, kt, ft, lt, mt, npairs = _pair_tables(nq, mode, bq, bk)
    has_x = qx is not None
    x_shared = has_x and kx.shape[1] == HEAD_DIM
    has_bias = bias is not None
    ones_col = dv == LANES
    W = G * HEAD_DIM

    def im(off_blocks, row_tab, nblk):
        def f(b, h, p, qt, kt, ft, lt, mt):
            t = qt if row_tab == "q" else kt
            return (b * nblk + t[p], off_blocks + h)
        return f

    in_specs = [
        pl.BlockSpec((bq, W), im(q_off // G, "q", nq)),
        pl.BlockSpec((bk, W), im(k_off // G, "k", nk)),
        pl.BlockSpec((bk, vw), im(v_off * v_group // G, "k", nk)),
    ]
    args = [q, k, v]
    if has_x:
        in_specs.append(pl.BlockSpec((bq, W), im(qx_off // G, "q", nq)))
        if x_shared:
            in_specs.append(pl.BlockSpec(
                (bk, HEAD_DIM), lambda b, h, p, qt, kt, ft, lt, mt: (b * nk + kt[p], 0)))
        else:
            in_specs.append(pl.BlockSpec((bk, W), im(0, "k", nk)))
        args += [qx, kx]
    if has_bias:
        in_specs.append(
            pl.BlockSpec((G, 1, bq, bk), lambda b, h, p, qt, kt, ft, lt, mt: (h, qt[p] - kt[p], 0, 0)))
        args.append(bias)

    kern = functools.partial(_attn_kernel, mode=mode, bq=bq, bk=bk, heads_per_step=G, dv=dv,
                             v_group=v_group, rsub=rsub, has_x=has_x, x_shared=x_shared,
                             has_bias=has_bias, ones_col=ones_col)
    return pl.pallas_call(
        kern,
        out_shape=jax.ShapeDtypeStruct((batch * seq, heads * dv), out_dtype),
        grid_spec=pltpu.PrefetchScalarGridSpec(
            num_scalar_prefetch=5,
            grid=(batch, heads // G, npairs),
            in_specs=in_specs,
            out_specs=pl.BlockSpec((bq, G * dv),
                                   lambda b, h, p, qt, kt, ft, lt, mt: (b * nq + qt[p], h)),
            scratch_shapes=[pltpu.VMEM((G, bq, LANES), F32),
                            pltpu.VMEM((G, bq, dv + LANES if ones_col else dv), F32)]
            + ([] if ones_col else [pltpu.VMEM((G, bq, LANES), F32)]),
        ),
        compiler_params=_cparams(("parallel", "parallel", "arbitrary")),
        name="attn_" + mode,
    )(qt, kt, ft, lt, mt, *args)


def _diff_combine_kernel(a_ref, lam_ref, g_ref, o_ref, *, lam_init):
    lf = lam_ref[...]
    lam = (jnp.exp(jnp.sum(lf[0:1] * lf[1:2], axis=-1, keepdims=True))
           - jnp.exp(jnp.sum(lf[2:3] * lf[3:4], axis=-1, keepdims=True)) + lam_init)
    w = 2 * DIFF_DIM
    for hd in range(H_DIFF):
        a1 = a_ref[:, (2 * hd) * w:(2 * hd + 1) * w]
        a2 = a_ref[:, (2 * hd + 1) * w:(2 * hd + 2) * w]
        d = a1 - lam * a2
        d = d * lax.rsqrt(jnp.mean(d * d, axis=-1, keepdims=True) + RMS_EPS) * g_ref[...]
        o_ref[:, hd * w:(hd + 1) * w] = (d * (1.0 - lam_init)).astype(o_ref.dtype)


def _diff_combine(a, diff_lambda, g_subln, lam_init, bm=512):
    M = a.shape[0]
    bm = min(bm, M)
    w = 2 * DIFF_DIM
    return pl.pallas_call(
        functools.partial(_diff_combine_kernel, lam_init=lam_init),
        out_shape=jax.ShapeDtypeStruct((M, H_DIFF * w), BF16),
        grid=(M // bm,),
        in_specs=[pl.BlockSpec((bm, 2 * H_DIFF * w), lambda i: (i, 0)),
                  pl.BlockSpec((4, DIFF_DIM), lambda i: (0, 0)),
                  pl.BlockSpec((1, w), lambda i: (0, 0))],
        out_specs=pl.BlockSpec((bm, H_DIFF * w), lambda i: (i, 0)),
        compiler_params=_cparams(("parallel",)),
        name="diff_combine",
    )(a, diff_lambda.astype(F32), g_subln.reshape(1, w).astype(F32))


PEER_SUB = 256


def _top_values(s, n):
    vals = []
    work = s
    for r in range(n):
        m = jnp.max(work, axis=0, keepdims=True)
        vals.append(m)
        if r + 1 < n:
            work = jnp.where(work >= m, -jnp.inf, work)
    return vals


def _peer_stats_kernel(q_ref, keys_ref, s2_ref, c_ref, tau_ref):
    nt = (((1,), (1,)), ((), ()))
    tb = q_ref.shape[0]
    k1 = keys_ref[0, 0]
    k2 = keys_ref[0, 1]
    for t0 in range(0, tb, PEER_SUB):
        q1 = q_ref[t0:t0 + PEER_SUB, 0:PEER_HALF]
        q2 = q_ref[t0:t0 + PEER_SUB, PEER_HALF:2 * PEER_HALF]
        s1 = lax.dot_general(k1, q1, nt, preferred_element_type=F32)
        s2 = lax.dot_general(k2, q2, nt, preferred_element_type=F32)
        sv1 = _top_values(s1, PEER_TOPK)
        sv2 = jnp.concatenate(_top_values(s2, PEER_TOPK), axis=0)
        rank = lax.broadcasted_iota(jnp.int32, (8, PEER_SUB), 0)
        sv2_lo = sv2[0:8]
        parts = [sv1[0] + sv2, sv1[1] + sv2_lo]
        for a in range(2, PEER_TOPK):
            parts.append(jnp.where(rank < PEER_TOPK // (a + 1), sv1[a] + sv2_lo, -jnp.inf))
        cand = jnp.concatenate(parts, axis=0)
        top = _top_values(cand, PEER_TOPK + 1)
        cmax = top[0]
        tau = 0.5 * (top[PEER_TOPK - 1] + top[PEER_TOPK])
        z = jnp.sum(jnp.where(cand >= tau, jnp.exp(cand - cmax), 0.0), axis=0, keepdims=True)
        lse = cmax + jnp.log(z)
        s2_ref[0, :, t0:t0 + PEER_SUB] = s2 * LOG2E
        c_ref[0, :, t0:t0 + PEER_SUB] = (s1 - lse) * LOG2E
        tau_ref[0, :, t0:t0 + PEER_SUB] = (tau - lse) * LOG2E


def _peer_stats(q, keys, tb=1024):
    T = q.shape[0]
    tb = min(tb, T)
    assert T % tb == 0 and tb % PEER_SUB == 0
    big = jax.ShapeDtypeStruct((PEER_HEADS, N_KEYS, T), F32)
    return pl.pallas_call(
        _peer_stats_kernel,
        out_shape=(big, big, jax.ShapeDtypeStruct((PEER_HEADS, 1, T), F32)),
        grid=(T // tb, PEER_HEADS),
        in_specs=[pl.BlockSpec((tb, 2 * PEER_HALF), lambda i, h: (i, h)),
                  pl.BlockSpec((1, 2, N_KEYS, PEER_HALF), lambda i, h: (h, 0, 0, 0))],
        out_specs=(pl.BlockSpec((1, N_KEYS, tb), lambda i, h: (h, 0, i)),
                   pl.BlockSpec((1, N_KEYS, tb), lambda i, h: (h, 0, i)),
                   pl.BlockSpec((1, 1, tb), lambda i, h: (h, 0, i))),
        compiler_params=_cparams(("parallel", "parallel")),
        name="peer_stats",
    )(q, keys)


GELU_K = math.sqrt(2.0 / math.pi)
GATE_ROWS = 32
MIX_TOK = 128
MIX_A_UNITS = 2
MIX_C_UNITS = 16


def _peer_mix_body(xT_ref, u_ref, vT_ref, s2_ref, tau_ref, o_ref, c_ref, c_off,
                   h_wr, h_rd, w_wr, w_rd, ni):
    tb = xT_ref.shape[1]
    d = vT_ref.shape[1]

    def gate_sub(il, t0, k0):
        ts = slice(t0, t0 + MIX_TOK)
        ks = slice(k0, k0 + GATE_ROWS)
        rows = slice(il * N_KEYS + k0, il * N_KEYS + k0 + GATE_ROWS)
        g = None
        for h in range(PEER_HEADS):
            z = s2_ref[h, ks, ts] + c_ref[h, c_off + il:c_off + il + 1, ts]
            e = jnp.where(z >= tau_ref[h, :, ts], jnp.exp2(z), 0.0)
            g = e if g is None else g + e
        x = h_rd[rows, ts]
        inner = x * ((x * x) * (0.044715 * GELU_K) + GELU_K)
        hx = 0.5 * x
        act = hx + hx * jnp.tanh(inner)
        w_wr[rows, ts] = (g * act).astype(BF16)

    ec = u_ref.shape[0]
    a_part = ec // MIX_A_UNITS
    c_part = d // MIX_C_UNITS
    for t0 in range(0, tb, 2 * MIX_TOK):
        tq = slice(t0, t0 + 2 * MIX_TOK)

        def unit_a(j):
            h_wr[j * a_part:(j + 1) * a_part, tq] = jnp.dot(
                u_ref[j * a_part:(j + 1) * a_part, :], xT_ref[:, tq], preferred_element_type=F32)

        def unit_c(j):
            o_ref[j * c_part:(j + 1) * c_part, tq] += jnp.dot(
                vT_ref[0, j * c_part:(j + 1) * c_part, :], w_rd[:, tq], preferred_element_type=F32)

        mxu = []
        c_per_a = MIX_C_UNITS // MIX_A_UNITS
        for j in range(MIX_A_UNITS):
            mxu.append(functools.partial(unit_a, j))
            mxu += [functools.partial(unit_c, j * c_per_a + i) for i in range(c_per_a)]
        vpu = [functools.partial(gate_sub, r, t0 + dt, k0)
               for r in range(ni) for dt in (0, MIX_TOK) for k0 in range(0, N_KEYS, GATE_ROWS)]
        done = 0
        for i, unit in enumerate(mxu):
            unit()
            upto = ((i + 1) * len(vpu)) // len(mxu)
            for v in vpu[done:upto]:
                v()
            done = upto


def _peer_mix_kernel(xT_ref, u_ref, vT_ref, s2_ref, c_ref, tau_ref, o_ref, h0, h1, w0, w1, *, ni, nc):
    s = pl.program_id(0)

    @pl.when(s == 0)
    def _():
        h1[...] = jnp.zeros(h1.shape, F32)
        w0[...] = jnp.zeros(w0.shape, BF16)

    @pl.when(lax.rem(jnp.maximum(s - 2, 0), nc) == 0)
    def _():
        o_ref[...] = jnp.zeros(o_ref.shape, F32)

    args = (xT_ref, u_ref, vT_ref, s2_ref, tau_ref, o_ref, c_ref)

    @pl.when(lax.rem(s, 2) == 0)
    def _():
        _peer_mix_body(*args, (8 - ni) % 8, h0, h1, w1, w0, ni)

    @pl.when(lax.rem(s, 2) == 1)
    def _():
        _peer_mix_body(*args, 0, h1, h0, w0, w1, ni)


def _peer_mix(xT, u, v, s2, c, tau, *, tb=512, ec=512):
    D, T = xT.shape
    E = u.shape[0]
    tb = min(tb, T)
    assert T % tb == 0 and E % ec == 0 and ec % N_KEYS == 0 and tb % (2 * MIX_TOK) == 0
    ni = ec // N_KEYS
    nc = E // ec
    assert ni in (4, 8) and nc % 2 == 0
    steps = (T // tb) * nc
    vT = v.reshape(nc, ec, D).transpose(0, 2, 1)
    once = pl.Buffered(1)

    def at(lag):
        t = lambda s: jnp.clip(s - lag, 0, steps - 1)
        return (lambda s: t(s) // nc), (lambda s: lax.rem(t(s), nc))

    (blk_a, chunk_a), (blk_b, chunk_b), (blk_c, chunk_c) = at(0), at(1), at(2)
    return pl.pallas_call(
        functools.partial(_peer_mix_kernel, ni=ni, nc=nc),
        out_shape=jax.ShapeDtypeStruct((D, T), F32),
        grid=(steps + 2,),
        in_specs=[pl.BlockSpec((D, tb), lambda s: (0, blk_a(s)), pipeline_mode=once),
                  pl.BlockSpec((ec, D), lambda s: (chunk_a(s), 0)),
                  pl.BlockSpec((1, D, ec), lambda s: (chunk_c(s), 0, 0)),
                  pl.BlockSpec((PEER_HEADS, N_KEYS, tb), lambda s: (0, 0, blk_b(s)), pipeline_mode=once),
                  pl.BlockSpec((PEER_HEADS, 8, tb), lambda s: (0, chunk_b(s) * ni // 8, blk_b(s))),
                  pl.BlockSpec((PEER_HEADS, 1, tb), lambda s: (0, 0, blk_b(s)), pipeline_mode=once)],
        out_specs=pl.BlockSpec((D, tb), lambda s: (0, blk_c(s))),
        scratch_shapes=[pltpu.VMEM((ec, tb), F32), pltpu.VMEM((ec, tb), F32),
                        pltpu.VMEM((ec, tb), BF16), pltpu.VMEM((ec, tb), BF16)],
        compiler_params=_cparams(("arbitrary",)),
        name="peer_mix",
    )(xT, u, vT, s2, c, tau)


def _rope_tables(seq, d):
    inv = ROPE_THETA ** (-jnp.arange(0, d, 2, dtype=F32) / d)
    ang = jnp.arange(seq, dtype=F32)[:, None] * inv[None, :]
    cos, sin = jnp.cos(ang), jnp.sin(ang)
    half = d // 2
    pad = jnp.zeros((seq, LANES - d), F32)
    t0 = jnp.concatenate([cos, cos, pad], axis=1)
    if d == LANES:
        return (half,), (t0, jnp.concatenate([-sin, sin], axis=1))
    zh = jnp.zeros((seq, half), F32)
    t1 = jnp.concatenate([-sin, zh, pad], axis=1)
    t2 = jnp.concatenate([zh, sin, pad], axis=1)
    return (LANES - half, half), (t0, t1, t2)


def _band_bias_tiles(rel_bias, bq, bk):
    assert bq == bk
    n = bq
    nd = (LEFT_CHUNKS * CHUNK + bq - 1) // bk + 1
    m = np.arange(2 * n)
    pats = []
    for dlt in range(nd):
        rel = np.where(m < n, dlt * n - m, dlt * n + 2 * n - m)
        ridx = np.clip(rel, -REL_MAX, REL_MAX) + REL_MAX
        pats.append(jnp.take(rel_bias.astype(F32), jnp.asarray(ridx), axis=1))
    pat = jnp.stack(pats, axis=1)[:, :, None, :]
    nh = pat.shape[0]

    def toeplitz_kernel(p_ref, o_ref):
        rows = jnp.broadcast_to(p_ref[0, 0], (n, 2 * n))
        o_ref[0, 0] = pltpu.roll(rows, 0, 1, stride=1, stride_axis=0)[:, :n]

    return pl.pallas_call(
        toeplitz_kernel,
        out_shape=jax.ShapeDtypeStruct((nh, nd, n, n), F32),
        grid=(nh, nd),
        in_specs=[pl.BlockSpec((1, 1, 1, 2 * n), lambda h, j: (h, j, 0, 0))],
        out_specs=pl.BlockSpec((1, 1, n, n), lambda h, j: (h, j, 0, 0)),
        compiler_params=_cparams(("parallel", "parallel")),
        name="band_bias",
    )(pat)


def _decay_columns(cum):
    def top_bits(a):
        bits = lax.bitcast_convert_type(a, jnp.uint32) & jnp.uint32(0xFFFF0000)
        return lax.bitcast_convert_type(bits, F32)

    hi_f = top_bits(cum)
    r1 = cum - hi_f
    mid_f = top_bits(r1)
    pieces = [p[..., None] for p in (hi_f, mid_f, r1 - mid_f)]
    lane = lax.broadcasted_iota(jnp.int32, cum.shape + (HEAD_DIM,), 2)

    def columns(first3, next3):
        out = jnp.zeros(cum.shape + (HEAD_DIM,), F32)
        for i in range(3):
            out = jnp.where(lane == i, first3[i], out)
            out = jnp.where(lane == 3 + i, next3[i], out)
        return out.astype(BF16).reshape(cum.shape[0], -1)

    ones = [jnp.float32(1.0)] * 3
    return columns(pieces, ones), columns(ones, [-p for p in pieces])


def _even_layer(xb, w_in, b_forget, rel_bias, w_out, batch, seq):
    hd = HEAD_DIM
    scale = hd ** -0.5 * LOG2E
    n = H_FOX * hd
    fq, fk, fv = w_in[:, 0:n], w_in[:, n:2 * n], w_in[:, 2 * n:3 * n]
    wf = w_in[:, 3 * n:3 * n + H_FOX]
    o = 3 * n + H_FOX
    m = H_CHK * hd
    cq, ck, cv = w_in[:, o:o + m], w_in[:, o + m:o + 2 * m], w_in[:, o + 2 * m:o + 3 * m]
    w_main = jnp.concatenate([fq, cq, fk, fv, ck, cv], axis=1).astype(BF16)
    col_scale = jnp.concatenate([jnp.full((n + m,), scale, F32), jnp.ones((2 * n + 2 * m,), F32)])
    proj = _mm(xb, w_main, out_dtype=BF16, col_scale=col_scale)

    wf_pad = jnp.pad(wf, ((0, 0), (0, LANES - H_FOX))).astype(BF16)
    logits = _mm(xb, wf_pad, out_dtype=F32)
    bias_row = jnp.pad(b_forget.astype(F32), (0, LANES - H_FOX)).reshape(1, LANES)
    cum = _gate_scan(logits, bias_row, batch, seq)
    qx, kx = _decay_columns(cum[:, :H_FOX] * LOG2E)

    nh = n // hd
    o_fox = _attention(proj, proj, proj, batch=batch, seq=seq, heads=H_FOX,
                       q_off=0, k_off=2 * nh, v_off=3 * nh, dv=hd, v_group=1, mode="causal",
                       bq=512, bk=512, out_dtype=BF16, qx=qx, kx=kx)
    bqb = min(256, seq)
    bias = _band_bias_tiles(rel_bias * LOG2E, bqb, bqb)
    o_chk = _attention(proj, proj, proj, batch=batch, seq=seq, heads=H_CHK,
                       q_off=nh, k_off=4 * nh, v_off=5 * nh, dv=hd, v_group=1, mode="band",
                       bq=bqb, bk=bqb, out_dtype=BF16, bias=bias)
    return o_fox, o_chk, w_out.astype(BF16)


def _odd_layer(xb, w_in, g_q_lora, g_kv_lora, w_uq, w_ukv, diff_lambda, g_subln, w_out,
               layer_idx, batch, seq):
    o1 = Q_LORA + KV_LORA
    o2 = o1 + MLA_ROPE
    nd = H_DIFF * 2 * DIFF_DIM
    w_lora = w_in[:, :o1].astype(BF16)
    w_kpe = jnp.pad(w_in[:, o1:o2], ((0, 0), (0, LANES - MLA_ROPE))).astype(BF16)
    w_dqk = w_in[:, o2:o2 + 2 * nd].astype(BF16)
    w_dv = w_in[:, o2 + 2 * nd:o2 + 3 * nd].astype(BF16)

    rope128 = _rope_tables(seq, DIFF_DIM)
    rope64 = _rope_tables(seq, MLA_ROPE)

    lora = _mm(xb, w_lora, out_dtype=F32)
    kpe = _mm(xb, w_kpe, out_dtype=BF16, rope=rope64)
    dscale = jnp.concatenate([jnp.full((nd,), DIFF_DIM ** -0.5 * LOG2E, F32), jnp.ones((nd,), F32)])
    dqk = _mm(xb, w_dqk, out_dtype=BF16, col_scale=dscale, rope=rope128)
    dv = _mm(xb, w_dv, out_dtype=BF16)

    mscale = (MLA_NOPE + MLA_ROPE) ** -0.5 * LOG2E
    wq = w_uq.reshape(Q_LORA, H_MLA, MLA_NOPE + MLA_ROPE)
    wq_nope = wq[:, :, :MLA_NOPE].reshape(Q_LORA, H_MLA * MLA_NOPE).astype(BF16)
    wq_rope = jnp.pad(wq[:, :, MLA_NOPE:], ((0, 0), (0, 0), (0, LANES - MLA_ROPE)))
    wq_rope = wq_rope.reshape(Q_LORA, H_MLA * LANES).astype(BF16)
    cq = lora[:, :Q_LORA]
    ckv = lora[:, Q_LORA:]
    ms = jnp.full((H_MLA * LANES,), mscale, F32)
    q_nope = _mm(cq, wq_nope, out_dtype=BF16, gain=g_q_lora, col_scale=ms)
    q_rope = _mm(cq, wq_rope, out_dtype=BF16, gain=g_q_lora, col_scale=ms, rope=rope64)
    wkv = w_ukv.reshape(KV_LORA, H_MLA, MLA_NOPE + MLA_V)
    wkv = jnp.concatenate([wkv[:, :, :MLA_NOPE].reshape(KV_LORA, H_MLA * MLA_NOPE),
                           wkv[:, :, MLA_NOPE:].reshape(KV_LORA, H_MLA * MLA_V)], axis=1)
    kv = _mm(ckv, wkv.astype(BF16), out_dtype=BF16, gain=g_kv_lora)

    o_mla = _attention(q_nope, kv, kv, batch=batch, seq=seq, heads=H_MLA,
                       q_off=0, k_off=0, v_off=H_MLA, dv=MLA_V, v_group=1, mode="chunk",
                       bq=512, bk=512, out_dtype=BF16, qx=q_rope, kx=kpe)
    a = _attention(dqk, dqk, dv, batch=batch, seq=seq, heads=2 * H_DIFF,
                   q_off=0, k_off=2 * H_DIFF, v_off=0, dv=2 * DIFF_DIM, v_group=2, mode="chunk",
                   bq=512, bk=512, out_dtype=F32)
    lam_init = 0.8 - 0.6 * math.exp(-0.3 * layer_idx)
    o_diff = _diff_combine(a, diff_lambda, g_subln, lam_init)
    return o_mla, o_diff, w_out.astype(BF16)


def _peer_layer(xb, xbT, w_query, sub_keys, u_tab, v_tab):
    q = _mm(xb, w_query.astype(BF16), out_dtype=BF16)
    s2, c, tau = _peer_stats(q, sub_keys.astype(BF16))
    return _peer_mix(xbT, u_tab.astype(BF16), v_tab.astype(BF16), s2, c, tau)


def kernel(x, w_in_even, b_forget, rel_bias, w_out_even, w_in_odd, g_q_lora, g_kv_lora, w_uq, w_ukv, diff_lambda, g_subln, w_out_odd, peer_w_query, peer_sub_keys, peer_u, peer_v, ln_mix_g, ln_mix_b, ln_ffn_g, ln_ffn_b):
    batch, seq, d = x.shape
    depth = ln_mix_g.shape[0]
    alpha = (2 * depth) ** 0.25
    xf = x.reshape(batch * seq, d).astype(F32)
    xb = xf.astype(BF16)
    for l in range(depth):
        i = l // 2
        if l % 2 == 0:
            oa, ob, wo = _even_layer(xb, w_in_even[i], b_forget[i], rel_bias[i], w_out_even[i],
                                     batch, seq)
        else:
            oa, ob, wo = _odd_layer(xb, w_in_odd[i], g_q_lora[i], g_kv_lora[i], w_uq[i], w_ukv[i],
                                    diff_lambda[i], g_subln[i], w_out_odd[i], l, batch, seq)
        xf, xb, xbT = _proj_res_ln(oa, ob, wo, xf, ln_mix_g[l], ln_mix_b[l], alpha)
        yT = _peer_layer(xb, xbT, peer_w_query[l], peer_sub_keys[l], peer_u[l], peer_v[l])
        xf, xb = _res_ln(xf, yT, ln_ffn_g[l], ln_ffn_b[l], alpha)
    return xf.reshape(batch, seq, d)
```
